```python
import math, functools
import jax, jax.numpy as jnp
from jax import lax
import numpy as np

D_MODEL = 1024
BATCH = 4
SEQ = 8192
DEPTH = 2
DEC_BATCH = 32
DEC_SEQ = 8
PAST_LEN = 16384
PAGE_SIZE = 128

N_HEADS = 8
HEAD_DIM = 64
V_DIM = 2 * HEAD_DIM
QK_WIDTH = N_HEADS * 2 * HEAD_DIM
ATTN_WIDTH = N_HEADS * V_DIM
ROPE_THETA = 10000.0
Q_BLOCK = 128
SSM_WIDTH = D_MODEL // 2
SSM_GROUP = 16
SSM_GROUPS = SSM_WIDTH // SSM_GROUP
SSM_STATE = 64
CONV_WIDTH = D_MODEL // 2
CONV_K = 3
N_BRANCH = 3
FFN_HIDDEN = ((8 * D_MODEL // 3 + 255) // 256) * 256
IN_COLS = N_BRANCH * D_MODEL + 2 * QK_WIDTH + ATTN_WIDTH + SSM_WIDTH + 3 * CONV_WIDTH
RMS_EPS = 1e-6

kernel_name = 'hybrid_s5_diffattn_shortconv_step'


def _rmsnorm(x, g):
    xf = x.astype(jnp.float32)
    y = xf * lax.rsqrt(jnp.mean(xf * xf, axis=-1, keepdims=True) + RMS_EPS)
    return (y * g.astype(jnp.float32)).astype(x.dtype)


def _in_split_points():
    sizes = (N_BRANCH * D_MODEL, QK_WIDTH, QK_WIDTH, ATTN_WIDTH, SSM_WIDTH, CONV_WIDTH, CONV_WIDTH, CONV_WIDTH)
    return [int(c) for c in np.cumsum(sizes)[:-1]]


def _rope(x, pos):
    half = HEAD_DIM // 2
    inv = ROPE_THETA ** (-jnp.arange(half, dtype=jnp.float32) / half)
    ang = pos.astype(jnp.float32)[:, None] * inv[None, :]
    cos = jnp.cos(ang)[:, None, None, :]
    sin = jnp.sin(ang)[:, None, None, :]
    xf = x.astype(jnp.float32)
    x1, x2 = xf[..., :half], xf[..., half:]
    return jnp.concatenate([x1 * cos - x2 * sin, x2 * cos + x1 * sin], axis=-1)


def _cdiag_combine(e1, e2):
    a1r, a1i, b1r, b1i = e1
    a2r, a2i, b2r, b2i = e2
    ar = a2r * a1r - a2i * a1i
    ai = a2r * a1i + a2i * a1r
    br = a2r * b1r - a2i * b1i + b2r
    bi = a2r * b1i + a2i * b1r + b2i
    return (ar, ai, br, bi)


def _s5_mixer(u, s0, a_re, a_im, log_dt, b_re, b_im, c_re, c_im, d_skip, w_glu, b_glu):
    bsz, L, _ = u.shape
    uf = u.astype(jnp.float32).reshape(bsz, L, SSM_GROUPS, SSM_GROUP)
    lr = a_re.astype(jnp.float32)
    li = a_im.astype(jnp.float32)
    dt = jnp.exp(log_dt.astype(jnp.float32))[:, None]
    mag = jnp.exp(lr * dt)
    abar_r = mag * jnp.cos(li * dt)
    abar_i = mag * jnp.sin(li * dt)
    den = lr * lr + li * li
    nr, ni = abar_r - 1.0, abar_i
    fr = (nr * lr + ni * li) / den
    fi = (ni * lr - nr * li) / den
    br = b_re.astype(jnp.float32)
    bi = b_im.astype(jnp.float32)
    bbar_r = fr[..., None] * br - fi[..., None] * bi
    bbar_i = fr[..., None] * bi + fi[..., None] * br
    bu_r = jnp.einsum('blgh,gph->lbgp', uf, bbar_r)
    bu_i = jnp.einsum('blgh,gph->lbgp', uf, bbar_i)
    s0r = s0[:, 0].astype(jnp.float32)
    s0i = s0[:, 1].astype(jnp.float32)
    bu_r = bu_r.at[0].add(abar_r * s0r - abar_i * s0i)
    bu_i = bu_i.at[0].add(abar_r * s0i + abar_i * s0r)
    ar = jnp.broadcast_to(abar_r, (L, 1, SSM_GROUPS, SSM_STATE))
    ai = jnp.broadcast_to(abar_i, (L, 1, SSM_GROUPS, SSM_STATE))
    _, _, sr, si = lax.associative_scan(_cdiag_combine, (ar, ai, bu_r, bu_i), axis=0)
    y = (jnp.einsum('lbgp,ghp->blgh', sr, c_re.astype(jnp.float32))
         - jnp.einsum('lbgp,ghp->blgh', si, c_im.astype(jnp.float32))
         + d_skip.astype(jnp.float32) * uf)
    y = jax.nn.gelu(y.reshape(bsz, L, SSM_WIDTH))
    y = y * jax.nn.sigmoid(y @ w_glu.astype(jnp.float32) + b_glu.astype(jnp.float32))
    s_last = jnp.stack([sr[-1], si[-1]], axis=1)
    return y, s_last


def _short_conv(xb, xc, xin, buf, conv_w):
    z = xc * xin
    L = z.shape[1]
    zp = jnp.concatenate([buf.astype(z.dtype), z], axis=1)
    conv = sum(conv_w[j] * zp[:, j:j + L] for j in range(CONV_K))
    return xb * conv, zp[:, L:]


def _prompt_attention(q, k, v, lam):
    bsz, L = q.shape[0], q.shape[1]
    key_pos = jnp.arange(L)
    vf = v.astype(jnp.float32)

    def one_block(i):
        qs = lax.dynamic_slice_in_dim(q, i * Q_BLOCK, Q_BLOCK, axis=1)
        s = jnp.einsum('bqhcd,bkhcd->bhcqk', qs, k)
        qpos = i * Q_BLOCK + jnp.arange(Q_BLOCK)
        s = jnp.where(qpos[:, None] >= key_pos[None, :], s, -jnp.inf)
        p = jax.nn.softmax(s, axis=-1)
        w = p[:, :, 0] - lam * p[:, :, 1]
        return jnp.einsum('bhqk,bkhe->bqhe', w, vf)

    o = lax.map(one_block, jnp.arange(L // Q_BLOCK))
    return o.transpose(1, 0, 2, 3, 4).reshape(bsz, L, N_HEADS, V_DIM)


def _online_update(carry, s, v):
    m, l, acc = carry
    m_new = jnp.maximum(m, jnp.max(s, axis=-1))
    corr = jnp.exp(m - m_new)
    p = jnp.exp(s - m_new[..., None])
    l = l * corr + jnp.sum(p, axis=-1)
    acc = acc * corr[..., None] + jnp.einsum('bhcqk,bkhe->bhcqe', p, v.astype(jnp.float32))
    return (m_new, l, acc)


def _sample_attention(q, k, v, lam, cache_k, cache_v, page_table, layer):
    bsz, nq = q.shape[0], q.shape[1]

    def page_step(carry, pages):
        kp = cache_k[layer, pages].reshape(bsz, PAGE_SIZE, N_HEADS, 2, HEAD_DIM).astype(jnp.float32)
        vp = cache_v[layer, pages]
        s = jnp.einsum('bqhcd,bkhcd->bhcqk', q, kp)
        return _online_update(carry, s, vp), None

    init = (jnp.full((bsz, N_HEADS, 2, nq), -jnp.inf, jnp.float32),
            jnp.zeros((bsz, N_HEADS, 2, nq), jnp.float32),
            jnp.zeros((bsz, N_HEADS, 2, nq, V_DIM), jnp.float32))
    carry, _ = lax.scan(page_step, init, page_table.T)
    s_new = jnp.einsum('bqhcd,bkhcd->bhcqk', q, k)
    causal = jnp.arange(nq)[:, None] >= jnp.arange(nq)[None, :]
    s_new = jnp.where(causal, s_new, -jnp.inf)
    m, l, acc = _online_update(carry, s_new, v)
    o = acc / l[..., None]
    o = o[:, :, 0] - lam * o[:, :, 1]
    return o.transpose(0, 2, 1, 3)


def _layer(x, pos, ssm_s0, conv_buf, attend, lw, lam_init):
    bsz, L, _ = x.shape
    h = _rmsnorm(x, lw['norm_mix'])
    proj = h @ lw['w_in']
    g, q, k, v, u, xb, xc, xin = jnp.split(proj, _in_split_points(), axis=-1)
    ya, s_new = _s5_mixer(u, ssm_s0, lw['ssm_a_re'], lw['ssm_a_im'], lw['ssm_log_dt'], lw['ssm_b_re'],
                          lw['ssm_b_im'], lw['ssm_c_re'], lw['ssm_c_im'], lw['ssm_d'], lw['w_glu'], lw['b_glu'])
    q = _rope(q.reshape(bsz, L, N_HEADS, 2, HEAD_DIM), pos) * (HEAD_DIM ** -0.5)
    k = _rope(k.reshape(bsz, L, N_HEADS, 2, HEAD_DIM), pos)
    v = v.reshape(bsz, L, N_HEADS, V_DIM)
    lam = (jnp.exp(jnp.sum(lw['lambda_q1'].astype(jnp.float32) * lw['lambda_k1'].astype(jnp.float32)))
           - jnp.exp(jnp.sum(lw['lambda_q2'].astype(jnp.float32) * lw['lambda_k2'].astype(jnp.float32)))
           + lam_init)
    o = attend(q, k, v, lam)
    yb = (_rmsnorm(o, lw['subln']) * (1.0 - lam_init)).reshape(bsz, L, ATTN_WIDTH)
    yc, conv_new = _short_conv(xb, xc, xin, conv_buf, lw['conv_w'])
    gate = jax.nn.sigmoid(g.astype(jnp.float32)).reshape(bsz, L, N_BRANCH, D_MODEL)
    merged = (gate[:, :, 0] * (ya @ lw['w_br_ssm'])
              + gate[:, :, 1] * (yb @ lw['w_br_attn'])
              + gate[:, :, 2] * (yc @ lw['w_br_conv']))
    x = x + (merged @ lw['w_out']).astype(x.dtype)
    h2 = _rmsnorm(x, lw['norm_ffn'])
    a, b = jnp.split(h2 @ lw['w_ffn_up'], [FFN_HIDDEN], axis=-1)
    x = x + ((jax.nn.silu(a) * b) @ lw['w_ffn_down']).astype(x.dtype)
    k_rows = k.reshape(bsz, L, N_HEADS, 2 * HEAD_DIM).astype(x.dtype)
    return x, k_rows, v, s_new, conv_new


def setup_inputs(seed: int = 0) -> dict:
    key = jax.random.key(seed)
    ks = jax.random.split(key, 40)
    f32 = jnp.float32
    n_pages = PAST_LEN // PAGE_SIZE
    n_pool = (DEC_BATCH * n_pages * 5) // 4
    nrm = lambda k, shape, s: jax.random.normal(k, shape, f32) * s
    x_prompt = nrm(ks[0], (BATCH, SEQ, D_MODEL), 1.0)
    x_sample = nrm(ks[1], (DEC_BATCH, DEC_SEQ, D_MODEL), 1.0)
    cache_k = nrm(ks[2], (DEPTH, n_pool, PAGE_SIZE, N_HEADS, 2 * HEAD_DIM), 1.0)
    cache_v = nrm(ks[3], (DEPTH, n_pool, PAGE_SIZE, N_HEADS, V_DIM), 1.0)
    state_ssm = nrm(ks[4], (DEPTH, DEC_BATCH, 2, SSM_GROUPS, SSM_STATE), 0.1)
    state_conv = nrm(ks[5], (DEPTH, DEC_BATCH, CONV_K - 1, CONV_WIDTH), 1.0)
    perm = jax.random.permutation(ks[6], n_pool)
    page_table = perm[:DEC_BATCH * n_pages].reshape(DEC_BATCH, n_pages).astype(jnp.int32)
    norm_mix = 1.0 + nrm(ks[7], (DEPTH, D_MODEL), 0.02)
    w_in = nrm(ks[8], (DEPTH, D_MODEL, IN_COLS), D_MODEL ** -0.5)
    ssm_a_re = -0.5 + nrm(ks[9], (DEPTH, SSM_GROUPS, SSM_STATE), 0.01)
    ssm_a_im = (jnp.pi * jnp.arange(SSM_STATE, dtype=f32))[None, None, :] + nrm(ks[10], (DEPTH, SSM_GROUPS, SSM_STATE), 0.01)
    ssm_log_dt = jax.random.uniform(ks[11], (DEPTH, SSM_GROUPS), f32, math.log(1e-3), math.log(1e-1))
    ssm_b_re = nrm(ks[12], (DEPTH, SSM_GROUPS, SSM_STATE, SSM_GROUP), (2.0 * SSM_GROUP) ** -0.5)
    ssm_b_im = nrm(ks[13], (DEPTH, SSM_GROUPS, SSM_STATE, SSM_GROUP), (2.0 * SSM_GROUP) ** -0.5)
    ssm_c_re = nrm(ks[14], (DEPTH, SSM_GROUPS, SSM_GROUP, SSM_STATE), (2.0 * SSM_STATE) ** -0.5)
    ssm_c_im = nrm(ks[15], (DEPTH, SSM_GROUPS, SSM_GROUP, SSM_STATE), (2.0 * SSM_STATE) ** -0.5)
    ssm_d = nrm(ks[16], (DEPTH, SSM_GROUPS, SSM_GROUP), 1.0)
    w_glu = nrm(ks[17], (DEPTH, SSM_WIDTH, SSM_WIDTH), SSM_WIDTH ** -0.5)
    b_glu = nrm(ks[18], (DEPTH, SSM_WIDTH), 0.01)
    lambda_q1 = nrm(ks[19], (DEPTH, HEAD_DIM), 0.1)
    lambda_k1 = nrm(ks[20], (DEPTH, HEAD_DIM), 0.1)
    lambda_q2 = nrm(ks[21], (DEPTH, HEAD_DIM), 0.1)
    lambda_k2 = nrm(ks[22], (DEPTH, HEAD_DIM), 0.1)
    subln = 1.0 + nrm(ks[23], (DEPTH, V_DIM), 0.02)
    conv_w = nrm(ks[24], (DEPTH, CONV_K, CONV_WIDTH), CONV_K ** -0.5)
    w_br_ssm = nrm(ks[25], (DEPTH, SSM_WIDTH, D_MODEL), SSM_WIDTH ** -0.5)
    w_br_attn = nrm(ks[26], (DEPTH, ATTN_WIDTH, D_MODEL), ATTN_WIDTH ** -0.5)
    w_br_conv = nrm(ks[27], (DEPTH, CONV_WIDTH, D_MODEL), CONV_WIDTH ** -0.5)
    w_out = nrm(ks[28], (DEPTH, D_MODEL, D_MODEL), D_MODEL ** -0.5)
    norm_ffn = 1.0 + nrm(ks[29], (DEPTH, D_MODEL), 0.02)
    w_ffn_up = nrm(ks[30], (DEPTH, D_MODEL, 2 * FFN_HIDDEN), D_MODEL ** -0.5)
    w_ffn_down = nrm(ks[31], (DEPTH, FFN_HIDDEN, D_MODEL), FFN_HIDDEN ** -0.5)
    norm_final = 1.0 + nrm(ks[32], (D_MODEL,), 0.02)
    return {'x_prompt': x_prompt, 'x_sample': x_sample, 'cache_k': cache_k, 'cache_v': cache_v,
            'state_ssm': state_ssm, 'state_conv': state_conv, 'page_table': page_table,
            'norm_mix': norm_mix, 'w_in': w_in, 'ssm_a_re': ssm_a_re, 'ssm_a_im': ssm_a_im,
            'ssm_log_dt': ssm_log_dt, 'ssm_b_re': ssm_b_re, 'ssm_b_im': ssm_b_im, 'ssm_c_re': ssm_c_re,
            'ssm_c_im': ssm_c_im, 'ssm_d': ssm_d, 'w_glu': w_glu, 'b_glu': b_glu,
            'lambda_q1': lambda_q1, 'lambda_k1': lambda_k1, 'lambda_q2': lambda_q2, 'lambda_k2': lambda_k2,
            'subln': subln, 'conv_w': conv_w, 'w_br_ssm': w_br_ssm, 'w_br_attn': w_br_attn,
            'w_br_conv': w_br_conv, 'w_out': w_out, 'norm_ffn': norm_ffn, 'w_ffn_up': w_ffn_up,
            'w_ffn_down': w_ffn_down, 'norm_final': norm_final}


def reference(x_prompt, x_sample, cache_k, cache_v, state_ssm, state_conv, page_table,
              norm_mix, w_in, ssm_a_re, ssm_a_im, ssm_log_dt, ssm_b_re, ssm_b_im, ssm_c_re, ssm_c_im,
              ssm_d, w_glu, b_glu, lambda_q1, lambda_k1, lambda_q2, lambda_k2, subln, conv_w,
              w_br_ssm, w_br_attn, w_br_conv, w_out, norm_ffn, w_ffn_up, w_ffn_down, norm_final):
    pos_p = jnp.arange(SEQ, dtype=jnp.int32)
    pos_s = PAST_LEN + jnp.arange(DEC_SEQ, dtype=jnp.int32)
    ssm0_p = jnp.zeros((BATCH, 2, SSM_GROUPS, SSM_STATE), jnp.float32)
    conv0_p = jnp.zeros((BATCH, CONV_K - 1, CONV_WIDTH), x_prompt.dtype)
    hp, hs = x_prompt, x_sample
    kp_l, vp_l, ks_l, vs_l, sp_l, ss_l, cp_l, cs_l = [], [], [], [], [], [], [], []
    for l in range(DEPTH):
        lw = {'norm_mix': norm_mix[l], 'w_in': w_in[l], 'ssm_a_re': ssm_a_re[l], 'ssm_a_im': ssm_a_im[l],
              'ssm_log_dt': ssm_log_dt[l], 'ssm_b_re': ssm_b_re[l], 'ssm_b_im': ssm_b_im[l],
              'ssm_c_re': ssm_c_re[l], 'ssm_c_im': ssm_c_im[l], 'ssm_d': ssm_d[l], 'w_glu': w_glu[l],
              'b_glu': b_glu[l], 'lambda_q1': lambda_q1[l], 'lambda_k1': lambda_k1[l],
              'lambda_q2': lambda_q2[l], 'lambda_k2': lambda_k2[l], 'subln': subln[l], 'conv_w': conv_w[l],
              'w_br_ssm': w_br_ssm[l], 'w_br_attn': w_br_attn[l], 'w_br_conv': w_br_conv[l],
              'w_out': w_out[l], 'norm_ffn': norm_ffn[l], 'w_ffn_up': w_ffn_up[l], 'w_ffn_down': w_ffn_down[l]}
        lam_init = 0.8 - 0.6 * math.exp(-0.3 * l)
        hp, kp, vp, sp, cp = _layer(hp, pos_p, ssm0_p, conv0_p, _prompt_attention, lw, lam_init)
        sample_attend = functools.partial(_sample_attention, cache_k=cache_k, cache_v=cache_v,
                                          page_table=page_table, layer=l)
        hs, k_s, v_s, s_s, c_s = _layer(hs, pos_s, state_ssm[l], state_conv[l], sample_attend, lw, lam_init)
        kp_l.append(kp); vp_l.append(vp); sp_l.append(sp); cp_l.append(cp)
        ks_l.append(k_s); vs_l.append(v_s); ss_l.append(s_s); cs_l.append(c_s)
    y_prompt = _rmsnorm(hp, norm_final)
    y_sample = _rmsnorm(hs, norm_final)
    return (y_prompt, y_sample,
            jnp.stack(kp_l), jnp.stack(vp_l), jnp.stack(ks_l), jnp.stack(vs_l),
            jnp.stack(sp_l), jnp.stack(ss_l), jnp.stack(cp_l), jnp.stack(cs_l))
```

```python
import functools
import math

import jax
import jax.numpy as jnp
from jax import lax
from jax.experimental import pallas as pl
from jax.experimental.pallas import tpu as pltpu

RMS_EPS = 1e-6
ROPE_THETA = 10000.0
N_BRANCH = 3
SSM_GROUP = 16
SSM_CHUNK = 16
LANES = 128
SUBLANES = 8
GROUPS_PER_BLOCK = LANES // SSM_GROUP
VMEM_LIMIT = 56 * 1024 * 1024
PAGES_PER_STEP = 8

F32 = jnp.float32
BF16 = jnp.bfloat16


def _cparams(sem):
    return pltpu.CompilerParams(dimension_semantics=sem, vmem_limit_bytes=VMEM_LIMIT)


def _resident(shape):
    nd = len(shape)
    return pl.BlockSpec(shape, lambda *_: (0,) * nd, pipeline_mode=pl.Buffered(1))


def _rms(x, g):
    return x * lax.rsqrt(jnp.mean(x * x, axis=-1, keepdims=True) + RMS_EPS) * g


def _dot(a, b):
    return jnp.dot(a, b, preferred_element_type=F32)


def _dot_nt(a, b):
    return lax.dot_general(a, b, (((1,), (1,)), ((), ())), preferred_element_type=F32)


def _rope_block(x, cos, sa, sb):
    quarter = x.shape[1] // 4
    up = pltpu.roll(x, x.shape[1] - quarter, 1)
    down = pltpu.roll(x, quarter, 1)
    return x * cos + up * sa + down * sb


def _inproj_kernel(x_ref, nw_ref, w_ref, cq_ref, qa_ref, qb_ref, ck_ref, ka_ref, kb_ref,
                   gate_ref, q_ref, kf_ref, kh_ref, vf_ref, vh_ref, u_ref, xb_ref, xc_ref, xin_ref,
                   *, widths, chunk):
    h = _rms(x_ref[...], nw_ref[...]).astype(BF16)
    gw, qw, kw, vw, uw, cw = widths

    def segment(col0, width, emit):
        for c0 in range(0, width, chunk):
            cs = min(chunk, width - c0)
            emit(c0, cs, _dot(h, w_ref[:, col0 + c0:col0 + c0 + cs]))

    def emit_gate(c0, cs, acc):
        gate_ref[:, c0:c0 + cs] = jax.nn.sigmoid(acc).astype(BF16)

    def emit_q(c0, cs, acc):
        for b0 in range(0, cs, LANES):
            r = _rope_block(acc[:, b0:b0 + LANES], cq_ref[...], qa_ref[...], qb_ref[...])
            q_ref[:, c0 + b0:c0 + b0 + LANES] = r.astype(BF16)

    def emit_k(c0, cs, acc):
        for b0 in range(0, cs, LANES):
            r = _rope_block(acc[:, b0:b0 + LANES], ck_ref[...], ka_ref[...], kb_ref[...])
            kf_ref[:, c0 + b0:c0 + b0 + LANES] = r
            kh_ref[:, c0 + b0:c0 + b0 + LANES] = r.astype(BF16)

    def emit_v(c0, cs, acc):
        vf_ref[:, c0:c0 + cs] = acc
        vh_ref[:, c0:c0 + cs] = acc.astype(BF16)

    def emit_to(ref):
        def emit(c0, cs, acc):
            ref[:, c0:c0 + cs] = acc
        return emit

    col = 0
    for width, emit in ((gw, emit_gate), (qw, emit_q), (kw, emit_k), (vw, emit_v), (uw, emit_to(u_ref)),
                        (cw, emit_to(xb_ref)), (cw, emit_to(xc_ref)), (cw, emit_to(xin_ref))):
        segment(col, width, emit)
        col += width


def _in_projection(x2, norm_w, w_in_bf, rope_q, rope_k, widths, tm):
    m, d = x2.shape
    gw, qw, kw, vw, uw, cw = widths
    ntab = rope_q[0].shape[0] // tm
    row = lambda w: pl.BlockSpec((tm, w), lambda i: (i, 0))
    tab = pl.BlockSpec((tm, LANES), lambda i: (i % ntab, 0))
    out_shape = (jax.ShapeDtypeStruct((m, gw), BF16), jax.ShapeDtypeStruct((m, qw), BF16),
                 jax.ShapeDtypeStruct((m, kw), F32), jax.ShapeDtypeStruct((m, kw), BF16),
                 jax.ShapeDtypeStruct((m, vw), F32), jax.ShapeDtypeStruct((m, vw), BF16),
                 jax.ShapeDtypeStruct((m, uw), F32), jax.ShapeDtypeStruct((m, cw), F32),
                 jax.ShapeDtypeStruct((m, cw), F32), jax.ShapeDtypeStruct((m, cw), F32))
    return pl.pallas_call(
        functools.partial(_inproj_kernel, widths=widths, chunk=1024),
        grid=(m // tm,),
        in_specs=[row(d), _resident((1, d)), _resident(w_in_bf.shape), tab, tab, tab, tab, tab, tab],
        out_specs=(row(gw), row(qw), row(kw), row(kw), row(vw), row(vw), row(uw), row(cw), row(cw), row(cw)),
        out_shape=out_shape,
        compiler_params=_cparams(("parallel",)),
        name="in_projection",
    )(x2, norm_w.reshape(1, d), w_in_bf, *rope_q, *rope_k)


def _rope_tables(pos, head_dim, scale):
    half = head_dim // 2
    inv = ROPE_THETA ** (-jnp.arange(half, dtype=F32) / half)
    ang = pos.astype(F32)[:, None] * inv[None, :]
    cos, sin = jnp.cos(ang) * scale, jnp.sin(ang) * scale
    zero = jnp.zeros_like(sin)
    cos_t = jnp.concatenate([cos, cos, cos, cos], axis=1)
    sa = jnp.concatenate([-sin, zero, -sin, zero], axis=1)
    sb = jnp.concatenate([zero, sin, zero, sin], axis=1)
    return cos_t, sa, sb


def _ssm_kernel(u_ref, m_ref, bm_ref, cm_ref, at_ref, d_ref, s0_ref, y_ref, sl_ref, e_ref, st_ref, s_ref,
                *, steps, independent):
    seg = pl.program_id(2)
    half = at_ref.shape[-1] // 2
    x = jnp.concatenate([u_ref[0, :, t, :] for t in range(steps)], axis=1)
    xh = x.astype(BF16)
    e = _dot(xh, bm_ref[0])
    ar, ai = at_ref[0, :, :half], at_ref[0, :, half:]
    if independent:
        s0 = s0_ref[0, 0]
        sr, si = s0[:, :half], s0[:, half:]
        st = s0
        sl_ref[0, 0, :, :half] = ar * sr - ai * si + e[:, :half]
        sl_ref[0, 0, :, half:] = ar * si + ai * sr + e[:, half:]
    else:
        @pl.when(seg == 0)
        def _():
            s_ref[...] = s0_ref[0, 0]

        e_ref[...] = e
        nc = e_ref.shape[0]

        def body(c, carry):
            sr, si = carry
            st_ref[pl.ds(c, 1), :half] = sr
            st_ref[pl.ds(c, 1), half:] = si
            ec = e_ref[pl.ds(c, 1), :]
            return ar * sr - ai * si + ec[:, :half], ar * si + ai * sr + ec[:, half:]

        sr, si = lax.fori_loop(0, nc, body, (s_ref[:, :half], s_ref[:, half:]), unroll=8)
        s_ref[:, :half] = sr
        s_ref[:, half:] = si
        st = st_ref[...]

        @pl.when(seg == pl.num_programs(2) - 1)
        def _():
            sl_ref[0, 0] = s_ref[...]

    y = _dot(xh, m_ref[0]) + _dot(st.astype(BF16), cm_ref[0]) + d_ref[0] * x
    for t in range(steps):
        y_ref[0, :, t, :] = y[:, t * LANES:(t + 1) * LANES]


def _ssm_operators(a_re, a_im, log_dt, b_re, b_im, c_re, c_im, d_skip, steps):
    hp = lax.Precision.HIGHEST
    g, p = a_re.shape
    nblk = g // GROUPS_PER_BLOCK
    gl = GROUPS_PER_BLOCK
    lr, li = a_re.astype(F32), a_im.astype(F32)
    dt = jnp.exp(log_dt.astype(F32))[:, None]
    mag = jnp.exp(lr * dt)
    abar_r, abar_i = mag * jnp.cos(li * dt), mag * jnp.sin(li * dt)
    den = lr * lr + li * li
    nr, ni = abar_r - 1.0, abar_i
    fr, fi = (nr * lr + ni * li) / den, (ni * lr - nr * li) / den
    br, bi = b_re.astype(F32), b_im.astype(F32)
    bbar_r = fr[..., None] * br - fi[..., None] * bi
    bbar_i = fr[..., None] * bi + fi[..., None] * br
    n = jnp.arange(steps + 1, dtype=F32)[:, None, None]
    pw_mag = jnp.exp(n * (lr * dt))
    pw_r, pw_i = pw_mag * jnp.cos(n * (li * dt)), pw_mag * jnp.sin(n * (li * dt))
    cr, ci = c_re.astype(F32), c_im.astype(F32)
    wr = pw_r[..., None] * bbar_r - pw_i[..., None] * bbar_i
    wi = pw_r[..., None] * bbar_i + pw_i[..., None] * bbar_r
    kern = (jnp.einsum('gop,tgpi->tgio', cr, wr[:steps], precision=hp)
            - jnp.einsum('gop,tgpi->tgio', ci, wi[:steps], precision=hp))
    tt = jnp.arange(steps)
    lag = tt[None, :] - tt[:, None]
    toe = jnp.where((lag >= 0)[:, :, None, None, None], kern[jnp.clip(lag, 0)], 0.0)
    toe = toe.reshape(steps, steps, nblk, gl, SSM_GROUP, SSM_GROUP)
    eye = jnp.eye(gl, dtype=F32)
    m_op = jnp.einsum('abngio,gh->nagibho', toe, eye).reshape(nblk, steps * LANES, steps * LANES)
    rev = steps - 1 - tt
    er = wr[rev].reshape(steps, nblk, gl, p, SSM_GROUP)
    ei = wi[rev].reshape(steps, nblk, gl, p, SSM_GROUP)
    b_re_op = jnp.einsum('tngpi,gh->ntgihp', er, eye).reshape(nblk, steps * LANES, gl * p)
    b_im_op = jnp.einsum('tngpi,gh->ntgihp', ei, eye).reshape(nblk, steps * LANES, gl * p)
    b_op = jnp.concatenate([b_re_op, b_im_op], axis=2)
    qr = cr[None] * pw_r[1:, :, None, :] - ci[None] * pw_i[1:, :, None, :]
    qi = -(cr[None] * pw_i[1:, :, None, :] + ci[None] * pw_r[1:, :, None, :])
    qr = qr.reshape(steps, nblk, gl, SSM_GROUP, p)
    qi = qi.reshape(steps, nblk, gl, SSM_GROUP, p)
    c_re_op = jnp.einsum('tngop,gh->ngptho', qr, eye).reshape(nblk, gl * p, steps * LANES)
    c_im_op = jnp.einsum('tngop,gh->ngptho', qi, eye).reshape(nblk, gl * p, steps * LANES)
    c_op = jnp.concatenate([c_re_op, c_im_op], axis=1)
    a_t = jnp.concatenate([pw_r[steps].reshape(nblk, 1, gl * p), pw_i[steps].reshape(nblk, 1, gl * p)], axis=2)
    d_op = jnp.tile(d_skip.astype(F32).reshape(nblk, 1, LANES), (1, 1, steps))
    return m_op.astype(BF16), b_op.astype(BF16), c_op.astype(BF16), a_t, d_op


def _state_to_blocks(s, nblk):
    lead = s.shape[:-3]
    g, p = s.shape[-2:]
    s = s.reshape(lead + (2, nblk, (g // nblk) * p))
    return jnp.concatenate([s[..., 0, :, :], s[..., 1, :, :]], axis=-1)


def _blocks_to_state(s, g, p):
    lead = s.shape[:-2]
    nblk = s.shape[-2]
    half = s.shape[-1] // 2
    s = jnp.stack([s[..., :half], s[..., half:]], axis=-3)
    return s.reshape(lead + (2, g, p))


def _ssm_mixer(u, s0, ops, steps, independent):
    m_op, b_op, c_op, a_t, d_op = ops
    bsz, seq, width = u.shape
    g, p = s0.shape[-2:]
    nblk = width // LANES
    sw = a_t.shape[-1]
    if independent:
        u4 = u.reshape(1, bsz, steps, width)
        s0b = jnp.swapaxes(_state_to_blocks(s0, nblk), 0, 1)[None]
        nb, nc, ncs = 1, bsz, bsz
    else:
        nc = seq // steps
        ncs = min(nc, 128)
        u4 = u.reshape(bsz, nc, steps, width)
        s0b = _state_to_blocks(s0, nblk)[:, :, None, :]
        nb = bsz
    srows = s0b.shape[2]
    blk = lambda a: pl.BlockSpec((1,) + a.shape[1:], lambda o, b, s: (o, 0, 0), pipeline_mode=pl.Buffered(1))
    y4, sl = pl.pallas_call(
        functools.partial(_ssm_kernel, steps=steps, independent=independent),
        grid=(nblk, nb, nc // ncs),
        in_specs=[pl.BlockSpec((1, ncs, steps, LANES), lambda o, b, s: (b, s, 0, o)),
                  blk(m_op), blk(b_op), blk(c_op), blk(a_t), blk(d_op),
                  pl.BlockSpec((1, 1, srows, sw), lambda o, b, s: (b, o, 0, 0))],
        out_specs=(pl.BlockSpec((1, ncs, steps, LANES), lambda o, b, s: (b, s, 0, o)),
                   pl.BlockSpec((1, 1, srows, sw), lambda o, b, s: (b, o, 0, 0))),
        out_shape=(jax.ShapeDtypeStruct(u4.shape, F32), jax.ShapeDtypeStruct(s0b.shape, F32)),
        scratch_shapes=[pltpu.VMEM((ncs, sw), F32), pltpu.VMEM((ncs, sw), F32), pltpu.VMEM((1, sw), F32)],
        compiler_params=_cparams(("arbitrary", "arbitrary", "arbitrary")),
        name="ssm_mixer",
    )(u4, m_op, b_op, c_op, a_t, d_op, s0b)
    if independent:
        s_last = _blocks_to_state(jnp.swapaxes(sl[0], 0, 1), g, p)
    else:
        s_last = _blocks_to_state(sl[:, :, 0, :], g, p)
    return y4.reshape(bsz, seq, width), s_last


def _conv_kernel(xb_ref, xc_ref, xin_ref, buf_ref, w_ref, y_ref, new_ref, z_ref, *, taps):
    i = pl.program_id(1)
    tm = xb_ref.shape[1]
    keep = taps - 1
    z = xc_ref[0] * xin_ref[0]

    @pl.when(i == 0)
    def _():
        z_ref[SUBLANES - keep:SUBLANES, :] = buf_ref[0]

    @pl.when(i > 0)
    def _():
        z_ref[SUBLANES - keep:SUBLANES, :] = z_ref[tm + SUBLANES - keep:tm + SUBLANES, :]

    z_ref[SUBLANES:, :] = z
    conv = w_ref[taps - 1:taps, :] * z
    for j in range(taps - 1):
        conv = conv + w_ref[j:j + 1, :] * z_ref[SUBLANES - keep + j:SUBLANES - keep + j + tm, :]
    y_ref[0] = (xb_ref[0] * conv).astype(BF16)

    @pl.when(i == pl.num_programs(1) - 1)
    def _():
        new_ref[0] = z_ref[tm + SUBLANES - keep:tm + SUBLANES, :]


def _short_conv(xb, xc, xin, buf, conv_w, tm):
    bsz, seq, width = xb.shape
    taps = conv_w.shape[0]
    row = pl.BlockSpec((1, tm, width), lambda b, i: (b, i, 0))
    st = pl.BlockSpec((1, taps - 1, width), lambda b, i: (b, 0, 0))
    return pl.pallas_call(
        functools.partial(_conv_kernel, taps=taps),
        grid=(bsz, seq // tm),
        in_specs=[row, row, row, st, pl.BlockSpec((taps, width), lambda b, i: (0, 0))],
        out_specs=(row, st),
        out_shape=(jax.ShapeDtypeStruct((bsz, seq, width), BF16), jax.ShapeDtypeStruct(buf.shape, F32)),
        scratch_shapes=[pltpu.VMEM((tm + SUBLANES, width), F32)],
        compiler_params=_cparams(("arbitrary", "arbitrary")),
        name="short_conv",
    )(xb, xc, xin, buf.astype(F32), conv_w.astype(F32))


def _finish_heads(o1, o2, lam, sub):
    od = o1 - lam * o2
    return od * lax.rsqrt(jnp.mean(od * od, axis=-1, keepdims=True) + RMS_EPS) * sub


def _prompt_attn_kernel(lam_ref, q_ref, k_ref, v_ref, sub_ref, o_ref, *, blk):
    qi = pl.program_id(2)
    q = q_ref[0]
    lane = lax.broadcasted_iota(jnp.int32, q.shape, 1)
    first = lane < (q.shape[1] // 2)
    zero = jnp.zeros_like(q)
    qq = jnp.concatenate([jnp.where(first, q, zero), jnp.where(first, zero, q)], axis=0)

    def step(j, carry, diagonal):
        m, l, acc = carry
        start = pl.multiple_of(j * blk, blk)
        kb = k_ref[0, pl.ds(start, blk), :]
        vb = v_ref[0, pl.ds(start, blk), :]
        s = _dot_nt(qq, kb)
        if diagonal:
            r = lax.broadcasted_iota(jnp.int32, s.shape, 0)
            c = lax.broadcasted_iota(jnp.int32, s.shape, 1)
            r = jnp.where(r >= blk, r - blk, r)
            s = jnp.where(c <= r, s, -jnp.inf)
        m_new = jnp.maximum(m, jnp.max(s, axis=1, keepdims=True))
        corr = jnp.exp(m - m_new)
        p = jnp.exp(s - m_new)
        l = l * corr + jnp.sum(p, axis=1, keepdims=True)
        acc = acc * corr + _dot(p.astype(BF16), vb)
        return m_new, l, acc

    init = (jnp.full((2 * blk, 1), -jnp.inf, F32), jnp.zeros((2 * blk, 1), F32),
            jnp.zeros((2 * blk, v_ref.shape[2]), F32))
    carry = lax.fori_loop(0, qi, lambda j, c: step(j, c, False), init)
    _, l, acc = step(qi, carry, True)
    o = acc / l
    o_ref[0] = _finish_heads(o[:blk], o[blk:], lam_ref[0], sub_ref[...]).astype(BF16)


def _prompt_attention(q, k, v, lam, sub, bsz, seq, heads):
    hw = q.shape[1] // heads
    blk = min(512, seq)
    q3, k3, v3 = (a.reshape(bsz, seq, heads * hw) for a in (q, k, v))
    whole = pl.BlockSpec((1, seq, hw), lambda b, h, i: (b, 0, h))
    tile = pl.BlockSpec((1, blk, hw), lambda b, h, i: (b, i, h))
    out = pl.pallas_call(
        functools.partial(_prompt_attn_kernel, blk=blk),
        grid=(bsz, heads, seq // blk),
        in_specs=[pl.BlockSpec(memory_space=pltpu.SMEM), tile, whole, whole,
                  pl.BlockSpec((1, hw), lambda b, h, i: (0, 0))],
        out_specs=tile,
        out_shape=jax.ShapeDtypeStruct((bsz, seq, heads * hw), BF16),
        compiler_params=_cparams(("parallel", "parallel", "arbitrary")),
        name="prompt_attention",
    )(lam, q3, k3, v3, sub)
    return out.reshape(bsz * seq, heads * hw)


def _sample_attn_kernel(pt_ref, lam_ref, qa_ref, kn_ref, vn_ref, sub_ref, *refs, pages, heads, nq):
    k_refs, v_refs = refs[:pages], refs[pages:2 * pages]
    o_ref, m_ref, l_ref, acc_ref = refs[2 * pages:]
    j = pl.program_id(1)
    qa = qa_ref[0]
    rows = qa.shape[0]

    @pl.when(j == 0)
    def _():
        m_ref[...] = jnp.full(m_ref.shape, -jnp.inf, F32)
        l_ref[...] = jnp.zeros(l_ref.shape, F32)
        acc_ref[...] = jnp.zeros(acc_ref.shape, F32)

    def update(s, vcat):
        m = m_ref[...]
        m_new = jnp.maximum(m, jnp.max(s, axis=1, keepdims=True))
        corr = jnp.exp(m - m_new)
        p = jnp.exp(s - m_new)
        l_ref[...] = l_ref[...] * corr + jnp.sum(p, axis=1, keepdims=True)
        acc_ref[...] = acc_ref[...] * corr + _dot(p.astype(BF16), vcat)
        m_ref[...] = m_new

    def head_mask(width):
        r = lax.broadcasted_iota(jnp.int32, (rows, width), 0)
        c = lax.broadcasted_iota(jnp.int32, (rows, width), 1)
        return r // (2 * nq), r % nq, c // heads, c % heads

    s = jnp.concatenate([_dot_nt(qa, k_refs[g][0, 0].astype(BF16)) for g in range(pages)], axis=1)
    rh, _, _, ch = head_mask(s.shape[1])
    s = jnp.where(rh == ch, s, -jnp.inf)
    update(s, jnp.concatenate([v_refs[g][0, 0].astype(BF16) for g in range(pages)], axis=0))

    @pl.when(j == pl.num_programs(1) - 1)
    def _():
        sn = _dot_nt(qa, kn_ref[0])
        rh, rq, ct, ch = head_mask(sn.shape[1])
        sn = jnp.where((rh == ch) & (ct <= rq), sn, -jnp.inf)
        update(sn, vn_ref[0])
        o = acc_ref[...] / l_ref[...]
        lam = lam_ref[0]
        for h in range(heads):
            o1 = o[(2 * h) * nq:(2 * h + 1) * nq]
            o2 = o[(2 * h + 1) * nq:(2 * h + 2) * nq]
            o_ref[0, :, h * LANES:(h + 1) * LANES] = _finish_heads(o1, o2, lam, sub_ref[...]).astype(BF16)


def _sample_attention(q, k, v, lam, sub, cache_k, cache_v, page_table, layer, bsz, nq, heads):
    hw = q.shape[1] // heads
    depth, n_pool, page = cache_k.shape[:3]
    n_pages = page_table.shape[1]
    pages = math.gcd(PAGES_PER_STEP, n_pages)
    rows = heads * 2 * nq
    prow = page * heads
    q4 = q.reshape(bsz, nq, heads, 2, hw // 2)
    sel = jnp.eye(2, dtype=q.dtype)
    qa = jnp.einsum('bqhcd,ce->bhcqed', q4, sel).reshape(bsz, rows, hw)
    pad = ((0, 0), (0, prow - nq * heads), (0, 0))
    kn = jnp.pad(k.reshape(bsz, nq * heads, hw), pad)
    vn = jnp.pad(v.reshape(bsz, nq * heads, hw), pad)
    ck = cache_k.reshape(depth, n_pool, prow, hw)
    cv = cache_v.reshape(depth, n_pool, prow, hw)
    pt = page_table.reshape(-1).astype(jnp.int32)

    def page_spec(g):
        return pl.BlockSpec((1, 1, prow, hw), lambda b, j, p: (layer, p[b * n_pages + j * pages + g], 0, 0))

    per_b = lambda r: pl.BlockSpec((1, r, hw), lambda b, j, p: (b, 0, 0))
    grid_spec = pltpu.PrefetchScalarGridSpec(
        num_scalar_prefetch=1,
        grid=(bsz, n_pages // pages),
        in_specs=[pl.BlockSpec(memory_space=pltpu.SMEM), per_b(rows), per_b(prow), per_b(prow),
                  pl.BlockSpec((1, hw), lambda b, j, p: (0, 0))]
                 + [page_spec(g) for g in range(pages)] + [page_spec(g) for g in range(pages)],
        out_specs=pl.BlockSpec((1, nq, heads * hw), lambda b, j, p: (b, 0, 0)),
        scratch_shapes=[pltpu.VMEM((rows, 1), F32), pltpu.VMEM((rows, 1), F32), pltpu.VMEM((rows, hw), F32)],
    )
    out = pl.pallas_call(
        functools.partial(_sample_attn_kernel, pages=pages, heads=heads, nq=nq),
        grid_spec=grid_spec,
        out_shape=jax.ShapeDtypeStruct((bsz, nq, heads * hw), BF16),
        compiler_params=_cparams(("parallel", "arbitrary")),
        name="sample_attention",
    )(pt, lam, qa, kn, vn, sub, *([ck] * pages), *([cv] * pages))
    return out.reshape(bsz * nq, heads * hw)


def _merge_kernel(x_ref, gate_ref, ya_ref, yb_ref, yc_ref, wg_ref, bg_ref, wa_ref, wb_ref, wc_ref, wo_ref, o_ref):
    d = x_ref.shape[1]
    y = jax.nn.gelu(ya_ref[...])
    ya = y * jax.nn.sigmoid(_dot(y.astype(BF16), wg_ref[...]) + bg_ref[...])
    merged = (gate_ref[:, 0:d].astype(F32) * _dot(ya.astype(BF16), wa_ref[...])
              + gate_ref[:, d:2 * d].astype(F32) * _dot(yb_ref[...], wb_ref[...])
              + gate_ref[:, 2 * d:3 * d].astype(F32) * _dot(yc_ref[...], wc_ref[...]))
    o_ref[...] = x_ref[...] + _dot(merged.astype(BF16), wo_ref[...])


def _merge(x2, gate, ya, yb, yc, w_glu, b_glu, w_a, w_b, w_c, w_o, tm):
    m, d = x2.shape
    row = lambda a: pl.BlockSpec((tm, a.shape[1]), lambda i: (i, 0))
    weights = (w_glu, b_glu, w_a, w_b, w_c, w_o)
    return pl.pallas_call(
        _merge_kernel,
        grid=(m // tm,),
        in_specs=[row(a) for a in (x2, gate, ya, yb, yc)] + [_resident(w.shape) for w in weights],
        out_specs=pl.BlockSpec((tm, d), lambda i: (i, 0)),
        out_shape=jax.ShapeDtypeStruct((m, d), F32),
        compiler_params=_cparams(("parallel",)),
        name="branch_merge",
    )(x2, gate, ya, yb, yc, *weights)


def _ffn_kernel(x_ref, nw_ref, wu_ref, wd_ref, fw_ref, o_ref, *, hidden, chunk, final_norm):
    x = x_ref[...]
    h = _rms(x, nw_ref[...]).astype(BF16)
    acc = x
    for c0 in range(0, hidden, chunk):
        a = _dot(h, wu_ref[:, c0:c0 + chunk])
        b = _dot(h, wu_ref[:, hidden + c0:hidden + c0 + chunk])
        acc = acc + _dot((jax.nn.silu(a) * b).astype(BF16), wd_ref[c0:c0 + chunk, :])
    o_ref[...] = _rms(acc, fw_ref[...]) if final_norm else acc


def _ffn(x2, norm_w, w_up, w_down, final_w, final_norm, tm):
    m, d = x2.shape
    hidden = w_down.shape[0]
    chunk = hidden // 2 if hidden % (2 * LANES) == 0 else hidden
    return pl.pallas_call(
        functools.partial(_ffn_kernel, hidden=hidden, chunk=chunk, final_norm=final_norm),
        grid=(m // tm,),
        in_specs=[pl.BlockSpec((tm, d), lambda i: (i, 0)), _resident((1, d)), _resident(w_up.shape),
                  _resident(w_down.shape), _resident((1, d))],
        out_specs=pl.BlockSpec((tm, d), lambda i: (i, 0)),
        out_shape=jax.ShapeDtypeStruct((m, d), F32),
        compiler_params=_cparams(("parallel",)),
        name="swiglu_ffn",
    )(x2, norm_w.reshape(1, d), w_up, w_down, final_w.reshape(1, d))


def _layer(x2, bsz, seq, pos_tables, ssm_s0, conv_buf, attend, lw, lam_init, ssm_steps, independent,
           final_w, final_norm, tm, tm_wide, conv_tm):
    d = x2.shape[1]
    gate, q, kf, kh, vf, vh, u, xb, xc, xin = _in_projection(
        x2, lw['norm_mix'], lw['w_in'], pos_tables[0], pos_tables[1], lw['widths'], tm)
    ypre, s_new = _ssm_mixer(u.reshape(bsz, seq, -1), ssm_s0, lw['ssm_ops'][ssm_steps], ssm_steps, independent)
    sub = (lw['subln'].astype(F32) * (1.0 - lam_init)).reshape(1, -1)
    yb = attend(q, kh, vh, lw['lam'], sub)
    cw = xb.shape[1]
    yc, conv_new = _short_conv(xb.reshape(bsz, seq, cw), xc.reshape(bsz, seq, cw), xin.reshape(bsz, seq, cw),
                               conv_buf, lw['conv_w'], conv_tm)
    x2 = _merge(x2, gate, ypre.reshape(bsz * seq, -1), yb, yc.reshape(bsz * seq, cw), lw['w_glu'], lw['b_glu'],
                lw['w_br_ssm'], lw['w_br_attn'], lw['w_br_conv'], lw['w_out'], tm_wide)
    x2 = _ffn(x2, lw['norm_ffn'], lw['w_ffn_up'], lw['w_ffn_down'], final_w, final_norm, tm_wide)
    return x2, kf, vf, s_new, conv_new


def kernel(x_prompt, x_sample, cache_k, cache_v, state_ssm, state_conv, page_table, norm_mix, w_in, ssm_a_re, ssm_a_im, ssm_log_dt, ssm_b_re, ssm_b_im, ssm_c_re, ssm_c_im, ssm_d, w_glu, b_glu, lambda_q1, lambda_k1, lambda_q2, lambda_k2, subln, conv_w, w_br_ssm, w_br_attn, w_br_conv, w_out, norm_ffn, w_ffn_up, w_ffn_down, norm_final):
    bsz, seq, d = x_prompt.shape
    dbsz, dseq, _ = x_sample.shape
    depth = w_in.shape[0]
    heads, kwid = cache_k.shape[3], cache_k.shape[4]
    head_dim = kwid // 2
    vdim = cache_v.shape[4]
    assert kwid == LANES and vdim == LANES, "head blocks must fill one 128-lane block"
    past = page_table.shape[1] * cache_k.shape[2]
    ssm_w, conv_width = w_glu.shape[1], conv_w.shape[2]
    widths = (N_BRANCH * d, heads * kwid, heads * kwid, heads * vdim, ssm_w, conv_width)
    assert sum(widths) + 2 * conv_width == w_in.shape[2]
    assert seq % SSM_CHUNK == 0

    tm_p = min(256, bsz * seq)
    tm_s = dbsz * dseq
    scale = head_dim ** -0.5
    pos_p = jnp.arange(seq, dtype=jnp.int32)
    pos_s = jnp.tile(past + jnp.arange(dseq, dtype=jnp.int32), dbsz)
    tabs_p = (_rope_tables(pos_p, head_dim, scale), _rope_tables(pos_p, head_dim, 1.0))
    tabs_s = (_rope_tables(pos_s, head_dim, scale), _rope_tables(pos_s, head_dim, 1.0))

    ssm0_p = jnp.zeros((bsz, 2) + state_ssm.shape[3:], F32)
    conv0_p = jnp.zeros((bsz,) + state_conv.shape[2:], F32)
    hp, hs = x_prompt.reshape(bsz * seq, d), x_sample.reshape(dbsz * dseq, d)
    outs = [[] for _ in range(8)]
    for l in range(depth):
        lam_init = 0.8 - 0.6 * math.exp(-0.3 * l)
        lam = (jnp.exp(jnp.sum(lambda_q1[l].astype(F32) * lambda_k1[l].astype(F32)))
               - jnp.exp(jnp.sum(lambda_q2[l].astype(F32) * lambda_k2[l].astype(F32))) + lam_init).reshape(1)
        ssm_args = (ssm_a_re[l], ssm_a_im[l], ssm_log_dt[l], ssm_b_re[l], ssm_b_im[l], ssm_c_re[l], ssm_c_im[l],
                    ssm_d[l])
        lw = {'norm_mix': norm_mix[l], 'w_in': w_in[l].astype(BF16), 'widths': widths, 'lam': lam,
              'ssm_ops': {steps: _ssm_operators(*ssm_args, steps) for steps in sorted({SSM_CHUNK, dseq})},
              'w_glu': w_glu[l].astype(BF16), 'b_glu': b_glu[l].astype(F32).reshape(1, -1),
              'subln': subln[l], 'conv_w': conv_w[l], 'w_br_ssm': w_br_ssm[l].astype(BF16),
              'w_br_attn': w_br_attn[l].astype(BF16), 'w_br_conv': w_br_conv[l].astype(BF16),
              'w_out': w_out[l].astype(BF16), 'norm_ffn': norm_ffn[l], 'w_ffn_up': w_ffn_up[l].astype(BF16),
              'w_ffn_down': w_ffn_down[l].astype(BF16)}
        last = l == depth - 1
        attend_p = functools.partial(_prompt_attention, bsz=bsz, seq=seq, heads=heads)
        hp, kp, vp, sp, cp = _layer(hp, bsz, seq, tabs_p, ssm0_p, conv0_p, attend_p, lw, lam_init, SSM_CHUNK, False,
                                    norm_final, last, tm_p, min(512, bsz * seq), min(512, seq))
        attend_s = functools.partial(_sample_attention, cache_k=cache_k, cache_v=cache_v, page_table=page_table,
                                     layer=l, bsz=dbsz, nq=dseq, heads=heads)
        hs, ks, vs, ss, cs = _layer(hs, dbsz, dseq, tabs_s, state_ssm[l], state_conv[l], attend_s, lw, lam_init,
                                    dseq, True, norm_final, last, tm_s, tm_s, dseq)
        for lst, val in zip(outs, (kp.reshape(bsz, seq, heads, kwid), vp.reshape(bsz, seq, heads, vdim),
                                   ks.reshape(dbsz, dseq, heads, kwid), vs.reshape(dbsz, dseq, heads, vdim),
                                   sp, ss, cp, cs)):
            lst.append(val)
    return (hp.reshape(bsz, seq, d), hs.reshape(dbsz, dseq, d)) + tuple(jnp.stack(o) for o in outs)
```

```python
import functools
import math

import jax
import jax.numpy as jnp
from jax import lax
from jax.experimental import pallas as pl
from jax.experimental.pallas import tpu as pltpu

RMS_EPS = 1e-6
ROPE_THETA = 10000.0
N_BRANCH = 3
SSM_GROUP = 16
SSM_CHUNK = 16
LANES = 128
SUBLANES = 8
GROUPS_PER_BLOCK = LANES // SSM_GROUP
VMEM_LIMIT = 56 * 1024 * 1024
PAGES_PER_STEP = 8
ATTN_BQ = 1024
ATTN_BK = 512
LOG2E = math.log2(math.e)

F32 = jnp.float32
BF16 = jnp.bfloat16


def _cparams(sem):
    return pltpu.CompilerParams(dimension_semantics=sem, vmem_limit_bytes=VMEM_LIMIT)


def _resident(shape):
    nd = len(shape)
    return pl.BlockSpec(shape, lambda *_: (0,) * nd, pipeline_mode=pl.Buffered(1))


def _rms(x, g):
    return x * lax.rsqrt(jnp.mean(x * x, axis=-1, keepdims=True) + RMS_EPS) * g


def _dot(a, b):
    return jnp.dot(a, b, preferred_element_type=F32)


def _dot_nt(a, b):
    return lax.dot_general(a, b, (((1,), (1,)), ((), ())), preferred_element_type=F32)


def _rope_block(x, cos, sa, sb):
    quarter = x.shape[1] // 4
    up = pltpu.roll(x, x.shape[1] - quarter, 1)
    down = pltpu.roll(x, quarter, 1)
    return x * cos + up * sa + down * sb


def _inproj_kernel(x_ref, nw_ref, w_ref, cq_ref, qa_ref, qb_ref, ck_ref, ka_ref, kb_ref, *refs, widths, chunk, heads):
    gate_ref, q_ref, kf_ref, kh_ref, vf_ref, vh_ref, u_ref, xb_ref, xc_ref, xin_ref = refs[-10:]
    tm = x_ref.shape[0]
    h = _rms(x_ref[...], nw_ref[...]).astype(BF16)
    gw, qw, kw, vw, uw, cw = widths

    def segment(col0, width, emit):
        for c0 in range(0, width, chunk):
            cs = min(chunk, width - c0)
            emit(c0, cs, _dot(h, w_ref[:, col0 + c0:col0 + c0 + cs]))

    def emit_gate(c0, cs, acc):
        gate_ref[:, c0:c0 + cs] = jax.nn.sigmoid(acc).astype(BF16)

    def emit_q(c0, cs, acc):
        for b0 in range(0, cs, LANES):
            r = _rope_block(acc[:, b0:b0 + LANES], cq_ref[...], qa_ref[...], qb_ref[...])
            q_ref[:, c0 + b0:c0 + b0 + LANES] = r.astype(BF16)

    def emit_k(c0, cs, acc):
        for b0 in range(0, cs, LANES):
            r = _rope_block(acc[:, b0:b0 + LANES], ck_ref[...], ka_ref[...], kb_ref[...])
            kf_ref[pl.ds((c0 + b0) // LANES, tm, stride=heads), :] = r
            kh_ref[:, c0 + b0:c0 + b0 + LANES] = r.astype(BF16)

    def emit_v(c0, cs, acc):
        for b0 in range(0, cs, LANES):
            vf_ref[pl.ds((c0 + b0) // LANES, tm, stride=heads), :] = acc[:, b0:b0 + LANES]
        vh_ref[:, c0:c0 + cs] = acc.astype(BF16)

    def emit_to(ref):
        def emit(c0, cs, acc):
            ref[:, c0:c0 + cs] = acc
        return emit

    col = 0
    for width, emit in ((gw, emit_gate), (qw, emit_q), (kw, emit_k), (vw, emit_v), (uw, emit_to(u_ref)),
                        (cw, emit_to(xb_ref)), (cw, emit_to(xc_ref)), (cw, emit_to(xin_ref))):
        segment(col, width, emit)
        col += width


def _in_projection(x2, norm_w, w_in_bf, rope_q, rope_k, widths, tm, heads, layer, depth, kv_prev):
    m, d = x2.shape
    gw, qw, kw, vw, uw, cw = widths
    ntab = rope_q[0].shape[0] // tm
    nt = m // tm
    row = lambda w: pl.BlockSpec((tm, w), lambda i: (i, 0))
    tab = pl.BlockSpec((tm, LANES), lambda i: (i % ntab, 0))
    kv_rows = pl.BlockSpec((tm * heads, LANES), lambda i: (layer * nt + i, 0))
    kv_shape = jax.ShapeDtypeStruct((depth * m * heads, LANES), F32)
    out_shape = (jax.ShapeDtypeStruct((m, gw), BF16), jax.ShapeDtypeStruct((m, qw), BF16),
                 kv_shape, jax.ShapeDtypeStruct((m, kw), BF16), kv_shape, jax.ShapeDtypeStruct((m, vw), BF16),
                 jax.ShapeDtypeStruct((m, uw), F32), jax.ShapeDtypeStruct((m, cw), F32),
                 jax.ShapeDtypeStruct((m, cw), F32), jax.ShapeDtypeStruct((m, cw), F32))
    n_in = 9
    return pl.pallas_call(
        functools.partial(_inproj_kernel, widths=widths, chunk=1024, heads=heads),
        grid=(nt,),
        in_specs=[row(d), _resident((1, d)), _resident(w_in_bf.shape), tab, tab, tab, tab, tab, tab]
                 + [pl.BlockSpec(memory_space=pl.ANY)] * 2,
        out_specs=(row(gw), row(qw), kv_rows, row(kw), kv_rows, row(vw), row(uw), row(cw), row(cw), row(cw)),
        out_shape=out_shape,
        input_output_aliases={n_in: 2, n_in + 1: 4},
        compiler_params=_cparams(("parallel",)),
        name="in_projection",
    )(x2, norm_w.reshape(1, d), w_in_bf, *rope_q, *rope_k, *kv_prev)


def _rope_tables(pos, head_dim, scale):
    half = head_dim // 2
    inv = ROPE_THETA ** (-jnp.arange(half, dtype=F32) / half)
    ang = pos.astype(F32)[:, None] * inv[None, :]
    cos, sin = jnp.cos(ang) * scale, jnp.sin(ang) * scale
    zero = jnp.zeros_like(sin)
    cos_t = jnp.concatenate([cos, cos, cos, cos], axis=1)
    sa = jnp.concatenate([-sin, zero, -sin, zero], axis=1)
    sb = jnp.concatenate([zero, sin, zero, sin], axis=1)
    return cos_t, sa, sb


def _ssm_kernel(u_ref, m_ref, bm_ref, cm_ref, at_ref, d_ref, s0_ref, y_ref, sl_ref, e_ref, st_ref, s_ref,
                *, steps, independent):
    seg = pl.program_id(2)
    half = at_ref.shape[-1] // 2
    x = jnp.concatenate([u_ref[0, :, t, :] for t in range(steps)], axis=1)
    xh = x.astype(BF16)
    e = _dot(xh, bm_ref[0])
    ar, ai = at_ref[0, :, :half], at_ref[0, :, half:]
    if independent:
        s0 = s0_ref[0, 0]
        sr, si = s0[:, :half], s0[:, half:]
        st = s0
        sl_ref[0, 0, :, :half] = ar * sr - ai * si + e[:, :half]
        sl_ref[0, 0, :, half:] = ar * si + ai * sr + e[:, half:]
    else:
        @pl.when(seg == 0)
        def _():
            s_ref[...] = s0_ref[0, 0]

        e_ref[...] = e
        nc = e_ref.shape[0]

        def body(c, carry):
            sr, si = carry
            st_ref[pl.ds(c, 1), :half] = sr
            st_ref[pl.ds(c, 1), half:] = si
            ec = e_ref[pl.ds(c, 1), :]
            return ar * sr - ai * si + ec[:, :half], ar * si + ai * sr + ec[:, half:]

        sr, si = lax.fori_loop(0, nc, body, (s_ref[:, :half], s_ref[:, half:]), unroll=8)
        s_ref[:, :half] = sr
        s_ref[:, half:] = si
        st = st_ref[...]

        @pl.when(seg == pl.num_programs(2) - 1)
        def _():
            sl_ref[0, 0] = s_ref[...]

    y = _dot(xh, m_ref[0]) + _dot(st.astype(BF16), cm_ref[0]) + d_ref[0] * x
    for t in range(steps):
        y_ref[0, :, t, :] = y[:, t * LANES:(t + 1) * LANES]


def _ssm_operators(a_re, a_im, log_dt, b_re, b_im, c_re, c_im, d_skip, steps):
    hp = lax.Precision.HIGHEST
    g, p = a_re.shape
    nblk = g // GROUPS_PER_BLOCK
    gl = GROUPS_PER_BLOCK
    lr, li = a_re.astype(F32), a_im.astype(F32)
    dt = jnp.exp(log_dt.astype(F32))[:, None]
    mag = jnp.exp(lr * dt)
    abar_r, abar_i = mag * jnp.cos(li * dt), mag * jnp.sin(li * dt)
    den = lr * lr + li * li
    nr, ni = abar_r - 1.0, abar_i
    fr, fi = (nr * lr + ni * li) / den, (ni * lr - nr * li) / den
    br, bi = b_re.astype(F32), b_im.astype(F32)
    bbar_r = fr[..., None] * br - fi[..., None] * bi
    bbar_i = fr[..., None] * bi + fi[..., None] * br
    n = jnp.arange(steps + 1, dtype=F32)[:, None, None]
    pw_mag = jnp.exp(n * (lr * dt))
    pw_r, pw_i = pw_mag * jnp.cos(n * (li * dt)), pw_mag * jnp.sin(n * (li * dt))
    cr, ci = c_re.astype(F32), c_im.astype(F32)
    wr = pw_r[..., None] * bbar_r - pw_i[..., None] * bbar_i
    wi = pw_r[..., None] * bbar_i + pw_i[..., None] * bbar_r
    kern = (jnp.einsum('gop,tgpi->tgio', cr, wr[:steps], precision=hp)
            - jnp.einsum('gop,tgpi->tgio', ci, wi[:steps], precision=hp))
    flat = kern.reshape(steps, -1)
    padded = jnp.concatenate([flat, jnp.zeros((steps + 1, flat.shape[1]), F32)], axis=0)
    toe = jnp.tile(padded, (steps, 1))[:steps * 2 * steps].reshape(steps, 2 * steps, -1)[:, :steps]
    toe = toe.reshape(steps, steps, nblk, gl, SSM_GROUP, SSM_GROUP)
    eye = jnp.eye(gl, dtype=F32)
    toe = jnp.transpose(toe, (2, 0, 3, 4, 1, 5))
    m_op = (toe[:, :, :, :, :, None, :] * eye[None, None, :, None, None, :, None]).astype(BF16)
    m_op = m_op.reshape(nblk, steps * LANES, steps * LANES)
    rev = steps - 1 - jnp.arange(steps)
    er = wr[rev].reshape(steps, nblk, gl, p, SSM_GROUP)
    ei = wi[rev].reshape(steps, nblk, gl, p, SSM_GROUP)
    b_re_op = jnp.einsum('tngpi,gh->ntgihp', er, eye).reshape(nblk, steps * LANES, gl * p)
    b_im_op = jnp.einsum('tngpi,gh->ntgihp', ei, eye).reshape(nblk, steps * LANES, gl * p)
    b_op = jnp.concatenate([b_re_op, b_im_op], axis=2)
    qr = cr[None] * pw_r[1:, :, None, :] - ci[None] * pw_i[1:, :, None, :]
    qi = -(cr[None] * pw_i[1:, :, None, :] + ci[None] * pw_r[1:, :, None, :])
    qr = qr.reshape(steps, nblk, gl, SSM_GROUP, p)
    qi = qi.reshape(steps, nblk, gl, SSM_GROUP, p)
    c_re_op = jnp.einsum('tngop,gh->ngptho', qr, eye).reshape(nblk, gl * p, steps * LANES)
    c_im_op = jnp.einsum('tngop,gh->ngptho', qi, eye).reshape(nblk, gl * p, steps * LANES)
    c_op = jnp.concatenate([c_re_op, c_im_op], axis=1)
    a_t = jnp.concatenate([pw_r[steps].reshape(nblk, 1, gl * p), pw_i[steps].reshape(nblk, 1, gl * p)], axis=2)
    d_op = jnp.tile(d_skip.astype(F32).reshape(nblk, 1, LANES), (1, 1, steps))
    return m_op, b_op.astype(BF16), c_op.astype(BF16), a_t, d_op


def _state_to_blocks(s, nblk):
    lead = s.shape[:-3]
    g, p = s.shape[-2:]
    s = s.reshape(lead + (2, nblk, (g // nblk) * p))
    return jnp.concatenate([s[..., 0, :, :], s[..., 1, :, :]], axis=-1)


def _blocks_to_state(s, g, p):
    lead = s.shape[:-2]
    nblk = s.shape[-2]
    half = s.shape[-1] // 2
    s = jnp.stack([s[..., :half], s[..., half:]], axis=-3)
    return s.reshape(lead + (2, g, p))


def _ssm_mixer(u, s0, ops, steps, independent):
    m_op, b_op, c_op, a_t, d_op = ops
    bsz, seq, width = u.shape
    g, p = s0.shape[-2:]
    nblk = width // LANES
    sw = a_t.shape[-1]
    if independent:
        u4 = u.reshape(1, bsz, steps, width)
        s0b = jnp.swapaxes(_state_to_blocks(s0, nblk), 0, 1)[None]
        nb, nc, ncs = 1, bsz, bsz
    else:
        nc = seq // steps
        ncs = min(nc, 128)
        u4 = u.reshape(bsz, nc, steps, width)
        s0b = _state_to_blocks(s0, nblk)[:, :, None, :]
        nb = bsz
    srows = s0b.shape[2]
    blk = lambda a: pl.BlockSpec((1,) + a.shape[1:], lambda o, b, s: (o, 0, 0), pipeline_mode=pl.Buffered(1))
    y4, sl = pl.pallas_call(
        functools.partial(_ssm_kernel, steps=steps, independent=independent),
        grid=(nblk, nb, nc // ncs),
        in_specs=[pl.BlockSpec((1, ncs, steps, LANES), lambda o, b, s: (b, s, 0, o)),
                  blk(m_op), blk(b_op), blk(c_op), blk(a_t), blk(d_op),
                  pl.BlockSpec((1, 1, srows, sw), lambda o, b, s: (b, o, 0, 0))],
        out_specs=(pl.BlockSpec((1, ncs, steps, LANES), lambda o, b, s: (b, s, 0, o)),
                   pl.BlockSpec((1, 1, srows, sw), lambda o, b, s: (b, o, 0, 0))),
        out_shape=(jax.ShapeDtypeStruct(u4.shape, F32), jax.ShapeDtypeStruct(s0b.shape, F32)),
        scratch_shapes=[pltpu.VMEM((ncs, sw), F32), pltpu.VMEM((ncs, sw), F32), pltpu.VMEM((1, sw), F32)],
        compiler_params=_cparams(("arbitrary", "arbitrary", "arbitrary")),
        name="ssm_mixer",
    )(u4, m_op, b_op, c_op, a_t, d_op, s0b)
    if independent:
        s_last = _blocks_to_state(jnp.swapaxes(sl[0], 0, 1), g, p)
    else:
        s_last = _blocks_to_state(sl[:, :, 0, :], g, p)
    return y4.reshape(bsz, seq, width), s_last


def _conv_kernel(xb_ref, xc_ref, xin_ref, buf_ref, w_ref, y_ref, new_ref, z_ref, *, taps):
    i = pl.program_id(1)
    tm = xb_ref.shape[1]
    keep = taps - 1
    z = xc_ref[0] * xin_ref[0]

    @pl.when(i == 0)
    def _():
        z_ref[SUBLANES - keep:SUBLANES, :] = buf_ref[0]

    @pl.when(i > 0)
    def _():
        z_ref[SUBLANES - keep:SUBLANES, :] = z_ref[tm + SUBLANES - keep:tm + SUBLANES, :]

    z_ref[SUBLANES:, :] = z
    conv = w_ref[taps - 1:taps, :] * z
    for j in range(taps - 1):
        conv = conv + w_ref[j:j + 1, :] * z_ref[SUBLANES - keep + j:SUBLANES - keep + j + tm, :]
    y_ref[0] = (xb_ref[0] * conv).astype(BF16)

    @pl.when(i == pl.num_programs(1) - 1)
    def _():
        new_ref[0] = z_ref[tm + SUBLANES - keep:tm + SUBLANES, :]


def _short_conv(xb, xc, xin, buf, conv_w, tm):
    bsz, seq, width = xb.shape
    taps = conv_w.shape[0]
    row = pl.BlockSpec((1, tm, width), lambda b, i: (b, i, 0))
    st = pl.BlockSpec((1, taps - 1, width), lambda b, i: (b, 0, 0))
    return pl.pallas_call(
        functools.partial(_conv_kernel, taps=taps),
        grid=(bsz, seq // tm),
        in_specs=[row, row, row, st, pl.BlockSpec((taps, width), lambda b, i: (0, 0))],
        out_specs=(row, st),
        out_shape=(jax.ShapeDtypeStruct((bsz, seq, width), BF16), jax.ShapeDtypeStruct(buf.shape, F32)),
        scratch_shapes=[pltpu.VMEM((tm + SUBLANES, width), F32)],
        compiler_params=_cparams(("arbitrary", "arbitrary")),
        name="short_conv",
    )(xb, xc, xin, buf.astype(F32), conv_w.astype(F32))


def _finish_heads(o1, o2, lam, sub):
    od = o1 - lam * o2
    return od * lax.rsqrt(jnp.mean(od * od, axis=-1, keepdims=True) + RMS_EPS) * sub


def _prompt_attn_kernel(lam_ref, q_ref, k_ref, v_ref, sub_ref, o_ref, vx_ref, qq_ref, s_ref, m_ref, acc_ref,
                        *, bq, bk):
    qi = pl.program_id(2)
    hw = v_ref.shape[2]

    @pl.when(qi == 0)
    def _():
        vx_ref[:, :hw] = v_ref[0]
        vx_ref[:, hw:] = jnp.ones((vx_ref.shape[0], hw), BF16)

    q = q_ref[0]
    lane = lax.broadcasted_iota(jnp.int32, q.shape, 1)
    first = lane < (q.shape[1] // 2)
    zero = jnp.zeros_like(q)
    qq_ref[...] = jnp.concatenate([jnp.where(first, q, zero), jnp.where(first, zero, q)], axis=0)

    def scores(j, slot):
        start = pl.multiple_of(j * bk, bk)
        s_ref[slot] = _dot_nt(qq_ref[...], k_ref[0, pl.ds(start, bk), :])

    def absorb(j, slot, diag):
        start = pl.multiple_of(j * bk, bk)
        s = s_ref[slot]
        if diag is not None:
            r = lax.broadcasted_iota(jnp.int32, s.shape, 0)
            c = lax.broadcasted_iota(jnp.int32, s.shape, 1)
            r = jnp.where(r >= bq, r - bq, r)
            s = jnp.where(c + diag * bk <= r, s, -jnp.inf)
        m = m_ref[...]
        m_new = jnp.maximum(m, jnp.max(s, axis=1, keepdims=True))
        m_ref[...] = m_new
        p = jnp.exp2(s - jnp.concatenate([m_new] * (bk // LANES), axis=1)).astype(BF16)
        corr = jnp.exp2(m - m_new)
        acc_ref[...] = (acc_ref[...] * jnp.concatenate([corr] * (2 * hw // LANES), axis=1)
                        + _dot(p, vx_ref[pl.ds(start, bk), :]))

    def body(jj, _):
        j = 2 * jj
        scores(j + 1, 1)
        absorb(j, 0, None)
        scores(j + 2, 0)
        absorb(j + 1, 1, None)
        return 0

    per = bq // bk
    m_ref[...] = jnp.full(m_ref.shape, -jnp.inf, F32)
    acc_ref[...] = jnp.zeros(acc_ref.shape, F32)
    scores(0, 0)
    lax.fori_loop(0, qi * (per // 2), body, 0)
    diag0 = qi * per
    for d in range(per):
        if d > 0:
            scores(diag0 + d, d % 2)
        absorb(diag0 + d, d % 2, d)
    acc = acc_ref[...]
    o = acc[:, :hw] / acc[:, hw:hw + 1]
    o_ref[0] = _finish_heads(o[:bq], o[bq:], lam_ref[0], sub_ref[...]).astype(BF16)


def _prompt_attention(q, k, v, lam, sub, bsz, seq, heads):
    hw = q.shape[1] // heads
    bq, bk = min(ATTN_BQ, seq), min(ATTN_BK, seq)
    assert (bq // bk) % 2 == 0, "the key sweep is unrolled in pairs of blocks"
    q3, k3, v3 = (a.reshape(bsz, seq, heads * hw) for a in (q, k, v))
    whole = pl.BlockSpec((1, seq, hw), lambda b, h, i: (b, 0, h))
    tile = pl.BlockSpec((1, bq, hw), lambda b, h, i: (b, i, h))
    out = pl.pallas_call(
        functools.partial(_prompt_attn_kernel, bq=bq, bk=bk),
        grid=(bsz, heads, seq // bq),
        in_specs=[pl.BlockSpec(memory_space=pltpu.SMEM), tile, whole, whole,
                  pl.BlockSpec((1, hw), lambda b, h, i: (0, 0))],
        out_specs=tile,
        out_shape=jax.ShapeDtypeStruct((bsz, seq, heads * hw), BF16),
        scratch_shapes=[pltpu.VMEM((seq, 2 * hw), BF16), pltpu.VMEM((2 * bq, hw), BF16),
                        pltpu.VMEM((2, 2 * bq, bk), F32), pltpu.VMEM((2 * bq, LANES), F32),
                        pltpu.VMEM((2 * bq, 2 * hw), F32)],
        compiler_params=_cparams(("parallel", "parallel", "arbitrary")),
        name="prompt_attention",
    )(lam, q3, k3, v3, sub)
    return out.reshape(bsz * seq, heads * hw)


def _sample_attn_kernel(pt_ref, lam_ref, qa_ref, kn_ref, vn_ref, sub_ref, bias_ref, biasn_ref, *refs,
                        pages, heads, nq):
    k_refs, v_refs = refs[:pages], refs[pages:2 * pages]
    o_ref, m_ref, l_ref, acc_ref = refs[2 * pages:]
    j = pl.program_id(1)
    qa = qa_ref[0]

    @pl.when(j == 0)
    def _():
        m_ref[...] = jnp.full(m_ref.shape, -jnp.inf, F32)
        l_ref[...] = jnp.zeros(l_ref.shape, F32)
        acc_ref[...] = jnp.zeros(acc_ref.shape, F32)

    def update(state, scores, values):
        m, l, acc = state
        nblk = scores[0].shape[1] // LANES
        blocks = [s[:, i * LANES:(i + 1) * LANES] for s in scores for i in range(nblk)]
        part = blocks[0]
        for blk in blocks[1:]:
            part = jnp.maximum(part, blk)
        m_new = jnp.maximum(m, jnp.max(part, axis=1, keepdims=True))
        corr = jnp.exp2(m - m_new)
        l, acc = l * corr, acc * corr
        m_wide = jnp.concatenate([m_new] * nblk, axis=1)
        for s, v in zip(scores, values):
            p = jnp.exp2(s - m_wide)
            for i in range(nblk):
                l = l + p[:, i * LANES:(i + 1) * LANES]
            acc = acc + _dot(p.astype(BF16), v)
        return m_new, l, acc

    state = update((m_ref[...], l_ref[...], acc_ref[...]),
                   [_dot_nt(qa, k_refs[g][0, 0].astype(BF16)) + bias_ref[...] for g in range(pages)],
                   [v_refs[g][0, 0].astype(BF16) for g in range(pages)])
    m_ref[...], l_ref[...], acc_ref[...] = state

    @pl.when(j == pl.num_programs(1) - 1)
    def _():
        _, l, acc = update(state, [_dot_nt(qa, kn_ref[0]) + biasn_ref[...]], [vn_ref[0]])
        o = acc / jnp.sum(l, axis=1, keepdims=True)
        lam = lam_ref[0]
        for h in range(heads):
            o1 = o[(2 * h) * nq:(2 * h + 1) * nq]
            o2 = o[(2 * h + 1) * nq:(2 * h + 2) * nq]
            o_ref[0, :, h * LANES:(h + 1) * LANES] = _finish_heads(o1, o2, lam, sub_ref[...]).astype(BF16)


def _sample_attention(q, k, v, lam, sub, cache_k, cache_v, page_table, layer, bsz, nq, heads):
    hw = q.shape[1] // heads
    depth, n_pool, page = cache_k.shape[:3]
    n_pages = page_table.shape[1]
    pages = math.gcd(PAGES_PER_STEP, n_pages)
    rows = heads * 2 * nq
    prow = page * heads
    q4 = q.reshape(bsz, nq, heads, 2, hw // 2)
    sel = jnp.eye(2, dtype=q.dtype)
    qa = jnp.einsum('bqhcd,ce->bhcqed', q4, sel).reshape(bsz, rows, hw)
    pad = ((0, 0), (0, prow - nq * heads), (0, 0))
    kn = jnp.pad(k.reshape(bsz, nq * heads, hw), pad)
    vn = jnp.pad(v.reshape(bsz, nq * heads, hw), pad)
    ck = cache_k.reshape(depth, n_pool, prow, hw)
    cv = cache_v.reshape(depth, n_pool, prow, hw)
    pt = page_table.reshape(-1).astype(jnp.int32)
    assert hw == LANES and nq * heads <= prow
    r = jnp.arange(rows, dtype=jnp.int32)[:, None]
    c = jnp.arange(prow, dtype=jnp.int32)[None, :]
    same_head = (r // (2 * nq)) == (c % heads)
    bias = jnp.where(same_head, 0.0, -jnp.inf).astype(F32)
    bias_new = jnp.where(same_head & ((c // heads) <= (r % nq)), 0.0, -jnp.inf).astype(F32)

    def page_spec(g):
        return pl.BlockSpec((1, 1, prow, hw), lambda b, j, p: (layer, p[b * n_pages + j * pages + g], 0, 0))

    per_b = lambda n: pl.BlockSpec((1, n, hw), lambda b, j, p: (b, 0, 0))
    const = lambda shape: pl.BlockSpec(shape, lambda b, j, p: (0, 0))
    grid_spec = pltpu.PrefetchScalarGridSpec(
        num_scalar_prefetch=1,
        grid=(bsz, n_pages // pages),
        in_specs=[pl.BlockSpec(memory_space=pltpu.SMEM), per_b(rows), per_b(prow), per_b(prow),
                  const((1, hw)), const((rows, prow)), const((rows, prow))]
                 + [page_spec(g) for g in range(pages)] + [page_spec(g) for g in range(pages)],
        out_specs=pl.BlockSpec((1, nq, heads * hw), lambda b, j, p: (b, 0, 0)),
        scratch_shapes=[pltpu.VMEM((rows, LANES), F32), pltpu.VMEM((rows, LANES), F32),
                        pltpu.VMEM((rows, hw), F32)],
    )
    out = pl.pallas_call(
        functools.partial(_sample_attn_kernel, pages=pages, heads=heads, nq=nq),
        grid_spec=grid_spec,
        out_shape=jax.ShapeDtypeStruct((bsz, nq, heads * hw), BF16),
        compiler_params=_cparams(("parallel", "arbitrary")),
        name="sample_attention",
    )(pt, lam, qa, kn, vn, sub, bias, bias_new, *([ck] * pages), *([cv] * pages))
    return out.reshape(bsz * nq, heads * hw)


def _merge_kernel(x_ref, gate_ref, ya_ref, yb_ref, yc_ref, wg_ref, bg_ref, wa_ref, wb_ref, wc_ref, wo_ref, o_ref):
    d = x_ref.shape[1]
    y = jax.nn.gelu(ya_ref[...])
    ya = y * jax.nn.sigmoid(_dot(y.astype(BF16), wg_ref[...]) + bg_ref[...])
    merged = (gate_ref[:, 0:d].astype(F32) * _dot(ya.astype(BF16), wa_ref[...])
              + gate_ref[:, d:2 * d].astype(F32) * _dot(yb_ref[...], wb_ref[...])
              + gate_ref[:, 2 * d:3 * d].astype(F32) * _dot(yc_ref[...], wc_ref[...]))
    o_ref[...] = x_ref[...] + _dot(merged.astype(BF16), wo_ref[...])


def _merge(x2, gate, ya, yb, yc, w_glu, b_glu, w_a, w_b, w_c, w_o, tm):
    m, d = x2.shape
    row = lambda a: pl.BlockSpec((tm, a.shape[1]), lambda i: (i, 0))
    weights = (w_glu, b_glu, w_a, w_b, w_c, w_o)
    return pl.pallas_call(
        _merge_kernel,
        grid=(m // tm,),
        in_specs=[row(a) for a in (x2, gate, ya, yb, yc)] + [_resident(w.shape) for w in weights],
        out_specs=pl.BlockSpec((tm, d), lambda i: (i, 0)),
        out_shape=jax.ShapeDtypeStruct((m, d), F32),
        compiler_params=_cparams(("parallel",)),
        name="branch_merge",
    )(x2, gate, ya, yb, yc, *weights)


def _ffn_kernel(x_ref, nw_ref, wu_ref, wd_ref, fw_ref, o_ref, *, hidden, chunk, final_norm):
    x = x_ref[...]
    h = _rms(x, nw_ref[...]).astype(BF16)
    acc = x
    for c0 in range(0, hidden, chunk):
        a = _dot(h, wu_ref[:, c0:c0 + chunk])
        b = _dot(h, wu_ref[:, hidden + c0:hidden + c0 + chunk])
        acc = acc + _dot((jax.nn.silu(a) * b).astype(BF16), wd_ref[c0:c0 + chunk, :])
    o_ref[...] = _rms(acc, fw_ref[...]) if final_norm else acc


def _ffn(x2, norm_w, w_up, w_down, final_w, final_norm, tm):
    m, d = x2.shape
    hidden = w_down.shape[0]
    chunk = hidden // 2 if hidden % (2 * LANES) == 0 else hidden
    return pl.pallas_call(
        functools.partial(_ffn_kernel, hidden=hidden, chunk=chunk, final_norm=final_norm),
        grid=(m // tm,),
        in_specs=[pl.BlockSpec((tm, d), lambda i: (i, 0)), _resident((1, d)), _resident(w_up.shape),
                  _resident(w_down.shape), _resident((1, d))],
        out_specs=pl.BlockSpec((tm, d), lambda i: (i, 0)),
        out_shape=jax.ShapeDtypeStruct((m, d), F32),
        compiler_params=_cparams(("parallel",)),
        name="swiglu_ffn",
    )(x2, norm_w.reshape(1, d), w_up, w_down, final_w.reshape(1, d))


def _layer(x2, bsz, seq, pos_tables, ssm_s0, conv_buf, attend, lw, lam_init, ssm_steps, independent,
           final_w, final_norm, tm, tm_wide, conv_tm, heads, layer, depth, kv_prev):
    gate, q, kf, kh, vf, vh, u, xb, xc, xin = _in_projection(
        x2, lw['norm_mix'], lw['w_in'], pos_tables[0], pos_tables[1], lw['widths'], tm, heads, layer, depth, kv_prev)
    ypre, s_new = _ssm_mixer(u.reshape(bsz, seq, -1), ssm_s0, lw['ssm_ops'][ssm_steps], ssm_steps, independent)
    sub = (lw['subln'].astype(F32) * (1.0 - lam_init)).reshape(1, -1)
    yb = attend(q, kh, vh, lw['lam'], sub)
    cw = xb.shape[1]
    yc, conv_new = _short_conv(xb.reshape(bsz, seq, cw), xc.reshape(bsz, seq, cw), xin.reshape(bsz, seq, cw),
                               conv_buf, lw['conv_w'], conv_tm)
    x2 = _merge(x2, gate, ypre.reshape(bsz * seq, -1), yb, yc.reshape(bsz * seq, cw), lw['w_glu'], lw['b_glu'],
                lw['w_br_ssm'], lw['w_br_attn'], lw['w_br_conv'], lw['w_out'], tm_wide)
    x2 = _ffn(x2, lw['norm_ffn'], lw['w_ffn_up'], lw['w_ffn_down'], final_w, final_norm, tm_wide)
    return x2, kf, vf, s_new, conv_new


def kernel(x_prompt, x_sample, cache_k, cache_v, state_ssm, state_conv, page_table, norm_mix, w_in, ssm_a_re, ssm_a_im, ssm_log_dt, ssm_b_re, ssm_b_im, ssm_c_re, ssm_c_im, ssm_d, w_glu, b_glu, lambda_q1, lambda_k1, lambda_q2, lambda_k2, subln, conv_w, w_br_ssm, w_br_attn, w_br_conv, w_out, norm_ffn, w_ffn_up, w_ffn_down, norm_final):
    bsz, seq, d = x_prompt.shape
    dbsz, dseq, _ = x_sample.shape
    depth = w_in.shape[0]
    heads, kwid = cache_k.shape[3], cache_k.shape[4]
    head_dim = kwid // 2
    vdim = cache_v.shape[4]
    assert kwid == LANES and vdim == LANES, "head blocks must fill one 128-lane block"
    past = page_table.shape[1] * cache_k.shape[2]
    ssm_w, conv_width = w_glu.shape[1], conv_w.shape[2]
    widths = (N_BRANCH * d, heads * kwid, heads * kwid, heads * vdim, ssm_w, conv_width)
    assert sum(widths) + 2 * conv_width == w_in.shape[2]
    assert seq % SSM_CHUNK == 0

    tm_p = min(256, bsz * seq)
    tm_s = dbsz * dseq
    scale = head_dim ** -0.5 * LOG2E
    pos_p = jnp.arange(seq, dtype=jnp.int32)
    pos_s = jnp.tile(past + jnp.arange(dseq, dtype=jnp.int32), dbsz)
    tabs_p = (_rope_tables(pos_p, head_dim, scale), _rope_tables(pos_p, head_dim, 1.0))
    tabs_s = (_rope_tables(pos_s, head_dim, scale), _rope_tables(pos_s, head_dim, 1.0))

    ssm0_p = jnp.zeros((bsz, 2) + state_ssm.shape[3:], F32)
    conv0_p = jnp.zeros((bsz,) + state_conv.shape[2:], F32)
    hp, hs = x_prompt.reshape(bsz * seq, d), x_sample.reshape(dbsz * dseq, d)
    outs = [[] for _ in range(4)]
    kv_p = tuple(jnp.zeros((depth * bsz * seq * heads, LANES), F32) for _ in range(2))
    kv_s = tuple(jnp.zeros((depth * dbsz * dseq * heads, LANES), F32) for _ in range(2))
    for l in range(depth):
        lam_init = 0.8 - 0.6 * math.exp(-0.3 * l)
        lam = (jnp.exp(jnp.sum(lambda_q1[l].astype(F32) * lambda_k1[l].astype(F32)))
               - jnp.exp(jnp.sum(lambda_q2[l].astype(F32) * lambda_k2[l].astype(F32))) + lam_init).reshape(1)
        ssm_args = (ssm_a_re[l], ssm_a_im[l], ssm_log_dt[l], ssm_b_re[l], ssm_b_im[l], ssm_c_re[l], ssm_c_im[l],
                    ssm_d[l])
        lw = {'norm_mix': norm_mix[l], 'w_in': w_in[l].astype(BF16), 'widths': widths, 'lam': lam,
              'ssm_ops': {steps: _ssm_operators(*ssm_args, steps) for steps in sorted({SSM_CHUNK, dseq})},
              'w_glu': w_glu[l].astype(BF16), 'b_glu': b_glu[l].astype(F32).reshape(1, -1),
              'subln': subln[l], 'conv_w': conv_w[l], 'w_br_ssm': w_br_ssm[l].astype(BF16),
              'w_br_attn': w_br_attn[l].astype(BF16), 'w_br_conv': w_br_conv[l].astype(BF16),
              'w_out': w_out[l].astype(BF16), 'norm_ffn': norm_ffn[l], 'w_ffn_up': w_ffn_up[l].astype(BF16),
              'w_ffn_down': w_ffn_down[l].astype(BF16)}
        last = l == depth - 1
        attend_p = functools.partial(_prompt_attention, bsz=bsz, seq=seq, heads=heads)
        hp, kp, vp, sp, cp = _layer(hp, bsz, seq, tabs_p, ssm0_p, conv0_p, attend_p, lw, lam_init, SSM_CHUNK, False,
                                    norm_final, last, tm_p, min(512, bsz * seq), min(512, seq), heads, l, depth, kv_p)
        attend_s = functools.partial(_sample_attention, cache_k=cache_k, cache_v=cache_v, page_table=page_table,
                                     layer=l, bsz=dbsz, nq=dseq, heads=heads)
        hs, ks, vs, ss, cs = _layer(hs, dbsz, dseq, tabs_s, state_ssm[l], state_conv[l], attend_s, lw, lam_init,
                                    dseq, True, norm_final, last, tm_s, tm_s, dseq, heads, l, depth, kv_s)
        kv_p, kv_s = (kp, vp), (ks, vs)
        for lst, val in zip(outs, (sp, ss, cp, cs)):
            lst.append(val)
    return ((hp.reshape(bsz, seq, d), hs.reshape(dbsz, dseq, d),
             kv_p[0].reshape(depth, bsz, seq, heads, kwid), kv_p[1].reshape(depth, bsz, seq, heads, vdim),
             kv_s[0].reshape(depth, dbsz, dseq, heads, kwid), kv_s[1].reshape(depth, dbsz, dseq, heads, vdim))
            + tuple(jnp.stack(o) for o in outs))
```

```python
import functools
import math

import jax
import jax.numpy as jnp
from jax import lax
from jax.experimental import pallas as pl
from jax.experimental.pallas import tpu as pltpu

RMS_EPS = 1e-6
ROPE_THETA = 10000.0
N_BRANCH = 3
SSM_GROUP = 16
SSM_CHUNK = 16
LANES = 128
SUBLANES = 8
GROUPS_PER_BLOCK = LANES // SSM_GROUP
VMEM_LIMIT = 56 * 1024 * 1024
PAGES_PER_STEP = 8
ATTN_BQ = 1024
ATTN_BK = 512
LOG2E = math.log2(math.e)

F32 = jnp.float32
BF16 = jnp.bfloat16


def _cparams(sem):
    return pltpu.CompilerParams(dimension_semantics=sem, vmem_limit_bytes=VMEM_LIMIT)


def _resident(shape):
    nd = len(shape)
    return pl.BlockSpec(shape, lambda *_: (0,) * nd, pipeline_mode=pl.Buffered(1))


def _rms(x, g):
    return x * lax.rsqrt(jnp.mean(x * x, axis=-1, keepdims=True) + RMS_EPS) * g


def _dot(a, b):
    return jnp.dot(a, b, preferred_element_type=F32)


def _dot_nt(a, b):
    return lax.dot_general(a, b, (((1,), (1,)), ((), ())), preferred_element_type=F32)


def _rope_block(x, cos, sa, sb):
    quarter = x.shape[1] // 4
    up = pltpu.roll(x, x.shape[1] - quarter, 1)
    down = pltpu.roll(x, quarter, 1)
    return x * cos + up * sa + down * sb


def _inproj_kernel(x_ref, nw_ref, w_ref, cq_ref, qa_ref, qb_ref, ck_ref, ka_ref, kb_ref, *refs, widths, chunk, heads):
    gate_ref, q_ref, kf_ref, kh_ref, vf_ref, vh_ref, u_ref, xb_ref, xc_ref, xin_ref = refs[-10:]
    tm = x_ref.shape[0]
    h = _rms(x_ref[...], nw_ref[...]).astype(BF16)
    gw, qw, kw, vw, uw, cw = widths

    def segment(col0, width, emit):
        for c0 in range(0, width, chunk):
            cs = min(chunk, width - c0)
            emit(c0, cs, _dot(h, w_ref[:, col0 + c0:col0 + c0 + cs]))

    def emit_gate(c0, cs, acc):
        gate_ref[:, c0:c0 + cs] = jax.nn.sigmoid(acc).astype(BF16)

    def emit_q(c0, cs, acc):
        for b0 in range(0, cs, LANES):
            r = _rope_block(acc[:, b0:b0 + LANES], cq_ref[...], qa_ref[...], qb_ref[...])
            q_ref[:, c0 + b0:c0 + b0 + LANES] = r.astype(BF16)

    def emit_k(c0, cs, acc):
        for b0 in range(0, cs, LANES):
            r = _rope_block(acc[:, b0:b0 + LANES], ck_ref[...], ka_ref[...], kb_ref[...])
            kf_ref[pl.ds((c0 + b0) // LANES, tm, stride=heads), :] = r
            kh_ref[:, c0 + b0:c0 + b0 + LANES] = r.astype(BF16)

    def emit_v(c0, cs, acc):
        for b0 in range(0, cs, LANES):
            vf_ref[pl.ds((c0 + b0) // LANES, tm, stride=heads), :] = acc[:, b0:b0 + LANES]
        vh_ref[:, c0:c0 + cs] = acc.astype(BF16)

    def emit_to(ref):
        def emit(c0, cs, acc):
            ref[:, c0:c0 + cs] = acc
        return emit

    col = 0
    for width, emit in ((gw, emit_gate), (qw, emit_q), (kw, emit_k), (vw, emit_v), (uw, emit_to(u_ref)),
                        (cw, emit_to(xb_ref)), (cw, emit_to(xc_ref)), (cw, emit_to(xin_ref))):
        segment(col, width, emit)
        col += width


def _in_projection(x2, norm_w, w_in_bf, rope_q, rope_k, widths, tm, heads, layer, depth, kv_prev):
    m, d = x2.shape
    gw, qw, kw, vw, uw, cw = widths
    ntab = rope_q[0].shape[0] // tm
    nt = m // tm
    row = lambda w: pl.BlockSpec((tm, w), lambda i: (i, 0))
    tab = pl.BlockSpec((tm, LANES), lambda i: (i % ntab, 0))
    kv_rows = pl.BlockSpec((tm * heads, LANES), lambda i: (layer * nt + i, 0))
    kv_shape = jax.ShapeDtypeStruct((depth * m * heads, LANES), F32)
    out_shape = (jax.ShapeDtypeStruct((m, gw), BF16), jax.ShapeDtypeStruct((m, qw), BF16),
                 kv_shape, jax.ShapeDtypeStruct((m, kw), BF16), kv_shape, jax.ShapeDtypeStruct((m, vw), BF16),
                 jax.ShapeDtypeStruct((m, uw), F32), jax.ShapeDtypeStruct((m, cw), F32),
                 jax.ShapeDtypeStruct((m, cw), F32), jax.ShapeDtypeStruct((m, cw), F32))
    n_in = 9
    return pl.pallas_call(
        functools.partial(_inproj_kernel, widths=widths, chunk=1024, heads=heads),
        grid=(nt,),
        in_specs=[row(d), _resident((1, d)), _resident(w_in_bf.shape), tab, tab, tab, tab, tab, tab]
                 + [pl.BlockSpec(memory_space=pl.ANY)] * 2,
        out_specs=(row(gw), row(qw), kv_rows, row(kw), kv_rows, row(vw), row(uw), row(cw), row(cw), row(cw)),
        out_shape=out_shape,
        input_output_aliases={n_in: 2, n_in + 1: 4},
        compiler_params=_cparams(("parallel",)),
        name="in_projection",
    )(x2, norm_w.reshape(1, d), w_in_bf, *rope_q, *rope_k, *kv_prev)


def _rope_tables(pos, head_dim, scale):
    half = head_dim // 2
    inv = ROPE_THETA ** (-jnp.arange(half, dtype=F32) / half)
    ang = pos.astype(F32)[:, None] * inv[None, :]
    cos, sin = jnp.cos(ang) * scale, jnp.sin(ang) * scale
    zero = jnp.zeros_like(sin)
    cos_t = jnp.concatenate([cos, cos, cos, cos], axis=1)
    sa = jnp.concatenate([-sin, zero, -sin, zero], axis=1)
    sb = jnp.concatenate([zero, sin, zero, sin], axis=1)
    return cos_t, sa, sb


def _ssm_kernel(u_ref, dk_ref, bm_ref, cm_ref, at_ref, d_ref, s0_ref, y_ref, sl_ref, m_ref, e_ref, st_ref, s_ref,
                *, steps, independent):
    seg = pl.program_id(2)
    half = at_ref.shape[-1] // 2

    @pl.when((pl.program_id(1) == 0) & (seg == 0))
    def _():
        m_ref[...] = jnp.zeros(m_ref.shape, BF16)
        for a in range(steps):
            for b in range(a, steps):
                m_ref[a * LANES:(a + 1) * LANES, b * LANES:(b + 1) * LANES] = dk_ref[0, b - a]

    x = jnp.concatenate([u_ref[0, :, t, :] for t in range(steps)], axis=1)
    xh = x.astype(BF16)
    e = _dot(xh, bm_ref[0])
    ar, ai = at_ref[0, :, :half], at_ref[0, :, half:]
    if independent:
        s0 = s0_ref[0, 0]
        sr, si = s0[:, :half], s0[:, half:]
        st = s0
        sl_ref[0, 0, :, :half] = ar * sr - ai * si + e[:, :half]
        sl_ref[0, 0, :, half:] = ar * si + ai * sr + e[:, half:]
    else:
        @pl.when(seg == 0)
        def _():
            s_ref[...] = s0_ref[0, 0]

        e_ref[...] = e
        nc = e_ref.shape[0]

        def body(c, carry):
            sr, si = carry
            st_ref[pl.ds(c, 1), :half] = sr
            st_ref[pl.ds(c, 1), half:] = si
            ec = e_ref[pl.ds(c, 1), :]
            return ar * sr - ai * si + ec[:, :half], ar * si + ai * sr + ec[:, half:]

        sr, si = lax.fori_loop(0, nc, body, (s_ref[:, :half], s_ref[:, half:]), unroll=8)
        s_ref[:, :half] = sr
        s_ref[:, half:] = si
        st = st_ref[...]

        @pl.when(seg == pl.num_programs(2) - 1)
        def _():
            sl_ref[0, 0] = s_ref[...]

    y = _dot(xh, m_ref[...]) + _dot_nt(st.astype(BF16), cm_ref[0]) + d_ref[0] * x
    for t in range(steps):
        y_ref[0, :, t, :] = y[:, t * LANES:(t + 1) * LANES]


def _ssm_operators(a_re, a_im, log_dt, b_re, b_im, c_re, c_im, d_skip, steps):
    hp = lax.Precision.HIGHEST
    g, p = a_re.shape
    nblk = g // GROUPS_PER_BLOCK
    gl = GROUPS_PER_BLOCK
    lr, li = a_re.astype(F32), a_im.astype(F32)
    dt = jnp.exp(log_dt.astype(F32))[:, None]
    mag = jnp.exp(lr * dt)
    abar_r, abar_i = mag * jnp.cos(li * dt), mag * jnp.sin(li * dt)
    den = lr * lr + li * li
    nr, ni = abar_r - 1.0, abar_i
    fr, fi = (nr * lr + ni * li) / den, (ni * lr - nr * li) / den
    br, bi = b_re.astype(F32), b_im.astype(F32)
    bbar_r = fr[..., None] * br - fi[..., None] * bi
    bbar_i = fr[..., None] * bi + fi[..., None] * br
    n = jnp.arange(steps + 1, dtype=F32)[:, None, None]
    pw_mag = jnp.exp(n * (lr * dt))
    pw_r, pw_i = pw_mag * jnp.cos(n * (li * dt)), pw_mag * jnp.sin(n * (li * dt))
    cr, ci = c_re.astype(F32), c_im.astype(F32)
    wr = pw_r[..., None] * bbar_r - pw_i[..., None] * bbar_i
    wi = pw_r[..., None] * bbar_i + pw_i[..., None] * bbar_r
    kern = (jnp.einsum('gop,tgpi->tgio', cr, wr[:steps], precision=hp)
            - jnp.einsum('gop,tgpi->tgio', ci, wi[:steps], precision=hp))

    def group_diagonal(x):
        t, w = x.shape[0], x.shape[-1]
        x = jnp.tile(x.reshape(t, nblk, LANES, w), (1, 1, 1, gl))
        row_g = jnp.arange(LANES)[:, None] // SSM_GROUP
        col_g = jnp.arange(gl * w)[None, :] // w
        return jnp.swapaxes(jnp.where(row_g == col_g, x, 0.0), 0, 1).astype(BF16)

    dk = group_diagonal(kern)
    rev = steps - 1 - jnp.arange(steps)
    b_op = jnp.concatenate([group_diagonal(jnp.swapaxes(wr[rev], 2, 3)), group_diagonal(jnp.swapaxes(wi[rev], 2, 3))],
                           axis=3).reshape(nblk, steps * LANES, 2 * gl * p)
    qr = cr[None] * pw_r[1:, :, None, :] - ci[None] * pw_i[1:, :, None, :]
    qi = -(cr[None] * pw_i[1:, :, None, :] + ci[None] * pw_r[1:, :, None, :])
    c_op = jnp.concatenate([group_diagonal(qr), group_diagonal(qi)], axis=3).reshape(nblk, steps * LANES, 2 * gl * p)
    a_t = jnp.concatenate([pw_r[steps].reshape(nblk, 1, gl * p), pw_i[steps].reshape(nblk, 1, gl * p)], axis=2)
    d_op = jnp.tile(d_skip.astype(F32).reshape(nblk, 1, LANES), (1, 1, steps))
    return dk, b_op, c_op, a_t, d_op


def _state_to_blocks(s, nblk):
    lead = s.shape[:-3]
    g, p = s.shape[-2:]
    s = s.reshape(lead + (2, nblk, (g // nblk) * p))
    return jnp.concatenate([s[..., 0, :, :], s[..., 1, :, :]], axis=-1)


def _blocks_to_state(s, g, p):
    lead = s.shape[:-2]
    nblk = s.shape[-2]
    half = s.shape[-1] // 2
    s = jnp.stack([s[..., :half], s[..., half:]], axis=-3)
    return s.reshape(lead + (2, g, p))


def _ssm_mixer(u, s0, ops, steps, independent):
    m_op, b_op, c_op, a_t, d_op = ops
    bsz, seq, width = u.shape
    g, p = s0.shape[-2:]
    nblk = width // LANES
    sw = a_t.shape[-1]
    if independent:
        u4 = u.reshape(1, bsz, steps, width)
        s0b = jnp.swapaxes(_state_to_blocks(s0, nblk), 0, 1)[None]
        nb, nc, ncs = 1, bsz, bsz
    else:
        nc = seq // steps
        ncs = min(nc, 128)
        u4 = u.reshape(bsz, nc, steps, width)
        s0b = _state_to_blocks(s0, nblk)[:, :, None, :]
        nb = bsz
    srows = s0b.shape[2]
    blk = lambda a: pl.BlockSpec((1,) + a.shape[1:], lambda o, b, s: (o,) + (0,) * (a.ndim - 1),
                                 pipeline_mode=pl.Buffered(1))
    y4, sl = pl.pallas_call(
        functools.partial(_ssm_kernel, steps=steps, independent=independent),
        grid=(nblk, nb, nc // ncs),
        in_specs=[pl.BlockSpec((1, ncs, steps, LANES), lambda o, b, s: (b, s, 0, o)),
                  blk(m_op), blk(b_op), blk(c_op), blk(a_t), blk(d_op),
                  pl.BlockSpec((1, 1, srows, sw), lambda o, b, s: (b, o, 0, 0))],
        out_specs=(pl.BlockSpec((1, ncs, steps, LANES), lambda o, b, s: (b, s, 0, o)),
                   pl.BlockSpec((1, 1, srows, sw), lambda o, b, s: (b, o, 0, 0))),
        out_shape=(jax.ShapeDtypeStruct(u4.shape, F32), jax.ShapeDtypeStruct(s0b.shape, F32)),
        scratch_shapes=[pltpu.VMEM((steps * LANES, steps * LANES), BF16), pltpu.VMEM((ncs, sw), F32),
                        pltpu.VMEM((ncs, sw), F32), pltpu.VMEM((1, sw), F32)],
        compiler_params=_cparams(("arbitrary", "arbitrary", "arbitrary")),
        name="ssm_mixer",
    )(u4, m_op, b_op, c_op, a_t, d_op, s0b)
    if independent:
        s_last = _blocks_to_state(jnp.swapaxes(sl[0], 0, 1), g, p)
    else:
        s_last = _blocks_to_state(sl[:, :, 0, :], g, p)
    return y4.reshape(bsz, seq, width), s_last


def _conv_kernel(xb_ref, xc_ref, xin_ref, buf_ref, w_ref, y_ref, new_ref, z_ref, *, taps):
    i = pl.program_id(1)
    tm = xb_ref.shape[1]
    keep = taps - 1
    z = xc_ref[0] * xin_ref[0]

    @pl.when(i == 0)
    def _():
        z_ref[SUBLANES - keep:SUBLANES, :] = buf_ref[0]

    @pl.when(i > 0)
    def _():
        z_ref[SUBLANES - keep:SUBLANES, :] = z_ref[tm + SUBLANES - keep:tm + SUBLANES, :]

    z_ref[SUBLANES:, :] = z
    conv = w_ref[taps - 1:taps, :] * z
    for j in range(taps - 1):
        conv = conv + w_ref[j:j + 1, :] * z_ref[SUBLANES - keep + j:SUBLANES - keep + j + tm, :]
    y_ref[0] = (xb_ref[0] * conv).astype(BF16)

    @pl.when(i == pl.num_programs(1) - 1)
    def _():
        new_ref[0] = z_ref[tm + SUBLANES - keep:tm + SUBLANES, :]


def _short_conv(xb, xc, xin, buf, conv_w, tm):
    bsz, seq, width = xb.shape
    taps = conv_w.shape[0]
    row = pl.BlockSpec((1, tm, width), lambda b, i: (b, i, 0))
    st = pl.BlockSpec((1, taps - 1, width), lambda b, i: (b, 0, 0))
    return pl.pallas_call(
        functools.partial(_conv_kernel, taps=taps),
        grid=(bsz, seq // tm),
        in_specs=[row, row, row, st, pl.BlockSpec((taps, width), lambda b, i: (0, 0))],
        out_specs=(row, st),
        out_shape=(jax.ShapeDtypeStruct((bsz, seq, width), BF16), jax.ShapeDtypeStruct(buf.shape, F32)),
        scratch_shapes=[pltpu.VMEM((tm + SUBLANES, width), F32)],
        compiler_params=_cparams(("arbitrary", "arbitrary")),
        name="short_conv",
    )(xb, xc, xin, buf.astype(F32), conv_w.astype(F32))


def _finish_heads(o1, o2, lam, sub):
    od = o1 - lam * o2
    return od * lax.rsqrt(jnp.mean(od * od, axis=-1, keepdims=True) + RMS_EPS) * sub


def _prompt_attn_kernel(lam_ref, q_ref, k_ref, v_ref, sub_ref, o_ref, vx_ref, qq_ref, s_ref, m_ref, acc_ref,
                        *, bq, bk):
    qi = pl.program_id(2)
    hw = v_ref.shape[2]

    @pl.when(qi == 0)
    def _():
        vx_ref[:, :hw] = v_ref[0]
        vx_ref[:, hw:] = jnp.ones((vx_ref.shape[0], hw), BF16)

    q = q_ref[0]
    lane = lax.broadcasted_iota(jnp.int32, q.shape, 1)
    first = lane < (q.shape[1] // 2)
    zero = jnp.zeros_like(q)
    qq_ref[...] = jnp.concatenate([jnp.where(first, q, zero), jnp.where(first, zero, q)], axis=0)

    def scores(j, slot):
        start = pl.multiple_of(j * bk, bk)
        s_ref[slot] = _dot_nt(qq_ref[...], k_ref[0, pl.ds(start, bk), :])

    def absorb(j, s, r0=0, triangular=False):
        start = pl.multiple_of(j * bk, bk)
        rows = pl.ds(r0, s.shape[0])
        if triangular:
            s = jnp.where(lax.broadcasted_iota(jnp.int32, s.shape, 1) <= lax.broadcasted_iota(jnp.int32, s.shape, 0),
                          s, -jnp.inf)
        m = m_ref[rows, :]
        m_new = jnp.maximum(m, jnp.max(s, axis=1, keepdims=True))
        m_ref[rows, :] = m_new
        p = jnp.exp2(s - jnp.concatenate([m_new] * (bk // LANES), axis=1)).astype(BF16)
        corr = jnp.exp2(m - m_new)
        acc_ref[rows, :] = (acc_ref[rows, :] * jnp.concatenate([corr] * (2 * hw // LANES), axis=1)
                            + _dot(p, vx_ref[pl.ds(start, bk), :]))

    def body(jj, _):
        j = 2 * jj
        scores(j + 1, 1)
        absorb(j, s_ref[0])
        scores(j + 2, 0)
        absorb(j + 1, s_ref[1])
        return 0

    m_ref[...] = jnp.full(m_ref.shape, -jnp.inf, F32)
    acc_ref[...] = jnp.zeros(acc_ref.shape, F32)
    scores(0, 0)
    lax.fori_loop(0, qi, body, 0)
    diag0 = 2 * qi
    for c in range(2):
        absorb(diag0, s_ref[0, c * bq:c * bq + bk, :], c * bq, triangular=True)
        absorb(diag0, s_ref[0, c * bq + bk:(c + 1) * bq, :], c * bq + bk)
    start = pl.multiple_of((diag0 + 1) * bk, bk)
    for c in range(2):
        r0 = c * bq + bk
        absorb(diag0 + 1, _dot_nt(qq_ref[r0:r0 + bk, :], k_ref[0, pl.ds(start, bk), :]), r0, triangular=True)
    acc = acc_ref[...]
    o = acc[:, :hw] / acc[:, hw:hw + 1]
    o_ref[0] = _finish_heads(o[:bq], o[bq:], lam_ref[0], sub_ref[...]).astype(BF16)


def _prompt_attention(q, k, v, lam, sub, bsz, seq, heads):
    hw = q.shape[1] // heads
    bq, bk = min(ATTN_BQ, seq), min(ATTN_BK, seq)
    assert bq == 2 * bk, "the key sweep and the diagonal handling work on pairs of key blocks"
    q3, k3, v3 = (a.reshape(bsz, seq, heads * hw) for a in (q, k, v))
    whole = pl.BlockSpec((1, seq, hw), lambda b, h, i: (b, 0, h))
    tile = pl.BlockSpec((1, bq, hw), lambda b, h, i: (b, i, h))
    out = pl.pallas_call(
        functools.partial(_prompt_attn_kernel, bq=bq, bk=bk),
        grid=(bsz, heads, seq // bq),
        in_specs=[pl.BlockSpec(memory_space=pltpu.SMEM), tile, whole, whole,
                  pl.BlockSpec((1, hw), lambda b, h, i: (0, 0))],
        out_specs=tile,
        out_shape=jax.ShapeDtypeStruct((bsz, seq, heads * hw), BF16),
        scratch_shapes=[pltpu.VMEM((seq, 2 * hw), BF16), pltpu.VMEM((2 * bq, hw), BF16),
                        pltpu.VMEM((2, 2 * bq, bk), F32), pltpu.VMEM((2 * bq, LANES), F32),
                        pltpu.VMEM((2 * bq, 2 * hw), F32)],
        compiler_params=_cparams(("parallel", "parallel", "arbitrary")),
        name="prompt_attention",
    )(lam, q3, k3, v3, sub)
    return out.reshape(bsz * seq, heads * hw)


def _sample_attn_kernel(pt_ref, lam_ref, qa_ref, kn_ref, vn_ref, sub_ref, bias_ref, biasn_ref, *refs,
                        pages, heads, nq):
    k_refs, v_refs = refs[:pages], refs[pages:2 * pages]
    o_ref, m_ref, l_ref, acc_ref = refs[2 * pages:]
    j = pl.program_id(1)
    qa = qa_ref[0]

    @pl.when(j == 0)
    def _():
        m_ref[...] = jnp.full(m_ref.shape, -jnp.inf, F32)
        l_ref[...] = jnp.zeros(l_ref.shape, F32)
        acc_ref[...] = jnp.zeros(acc_ref.shape, F32)

    def update(state, scores, values):
        m, l, acc = state
        nblk = scores[0].shape[1] // LANES
        blocks = [s[:, i * LANES:(i + 1) * LANES] for s in scores for i in range(nblk)]
        part = blocks[0]
        for blk in blocks[1:]:
            part = jnp.maximum(part, blk)
        m_new = jnp.maximum(m, jnp.max(part, axis=1, keepdims=True))
        corr = jnp.exp2(m - m_new)
        l, acc = l * corr, acc * corr
        m_wide = jnp.concatenate([m_new] * nblk, axis=1)
        for s, v in zip(scores, values):
            p = jnp.exp2(s - m_wide)
            for i in range(nblk):
                l = l + p[:, i * LANES:(i + 1) * LANES]
            acc = acc + _dot(p.astype(BF16), v)
        return m_new, l, acc

    state = update((m_ref[...], l_ref[...], acc_ref[...]),
                   [_dot_nt(qa, k_refs[g][0, 0].astype(BF16)) + bias_ref[...] for g in range(pages)],
                   [v_refs[g][0, 0].astype(BF16) for g in range(pages)])
    m_ref[...], l_ref[...], acc_ref[...] = state

    @pl.when(j == pl.num_programs(1) - 1)
    def _():
        _, l, acc = update(state, [_dot_nt(qa, kn_ref[0]) + biasn_ref[...]], [vn_ref[0]])
        o = acc / jnp.sum(l, axis=1, keepdims=True)
        lam = lam_ref[0]
        for h in range(heads):
            o1 = o[(2 * h) * nq:(2 * h + 1) * nq]
            o2 = o[(2 * h + 1) * nq:(2 * h + 2) * nq]
            o_ref[0, :, h * LANES:(h + 1) * LANES] = _finish_heads(o1, o2, lam, sub_ref[...]).astype(BF16)


def _sample_attention(q, k, v, lam, sub, cache_k, cache_v, page_table, layer, bsz, nq, heads):
    hw = q.shape[1] // heads
    depth, n_pool, page = cache_k.shape[:3]
    n_pages = page_table.shape[1]
    pages = math.gcd(PAGES_PER_STEP, n_pages)
    rows = heads * 2 * nq
    prow = page * heads
    q4 = q.reshape(bsz, nq, heads, 2, hw // 2)
    sel = jnp.eye(2, dtype=q.dtype)
    qa = jnp.einsum('bqhcd,ce->bhcqed', q4, sel).reshape(bsz, rows, hw)
    pad = ((0, 0), (0, prow - nq * heads), (0, 0))
    kn = jnp.pad(k.reshape(bsz, nq * heads, hw), pad)
    vn = jnp.pad(v.reshape(bsz, nq * heads, hw), pad)
    ck = cache_k.reshape(depth, n_pool, prow, hw)
    cv = cache_v.reshape(depth, n_pool, prow, hw)
    pt = page_table.reshape(-1).astype(jnp.int32)
    assert hw == LANES and nq * heads <= prow
    r = jnp.arange(rows, dtype=jnp.int32)[:, None]
    c = jnp.arange(prow, dtype=jnp.int32)[None, :]
    same_head = (r // (2 * nq)) == (c % heads)
    bias = jnp.where(same_head, 0.0, -jnp.inf).astype(F32)
    bias_new = jnp.where(same_head & ((c // heads) <= (r % nq)), 0.0, -jnp.inf).astype(F32)

    def page_spec(g):
        return pl.BlockSpec((1, 1, prow, hw), lambda b, j, p: (layer, p[b * n_pages + j * pages + g], 0, 0))

    per_b = lambda n: pl.BlockSpec((1, n, hw), lambda b, j, p: (b, 0, 0))
    const = lambda shape: pl.BlockSpec(shape, lambda b, j, p: (0, 0))
    grid_spec = pltpu.PrefetchScalarGridSpec(
        num_scalar_prefetch=1,
        grid=(bsz, n_pages // pages),
        in_specs=[pl.BlockSpec(memory_space=pltpu.SMEM), per_b(rows), per_b(prow), per_b(prow),
                  const((1, hw)), const((rows, prow)), const((rows, prow))]
                 + [page_spec(g) for g in range(pages)] + [page_spec(g) for g in range(pages)],
        out_specs=pl.BlockSpec((1, nq, heads * hw), lambda b, j, p: (b, 0, 0)),
        scratch_shapes=[pltpu.VMEM((rows, LANES), F32), pltpu.VMEM((rows, LANES), F32),
                        pltpu.VMEM((rows, hw), F32)],
    )
    out = pl.pallas_call(
        functools.partial(_sample_attn_kernel, pages=pages, heads=heads, nq=nq),
        grid_spec=grid_spec,
        out_shape=jax.ShapeDtypeStruct((bsz, nq, heads * hw), BF16),
        compiler_params=_cparams(("parallel", "arbitrary")),
        name="sample_attention",
    )(pt, lam, qa, kn, vn, sub, bias, bias_new, *([ck] * pages), *([cv] * pages))
    return out.reshape(bsz * nq, heads * hw)


def _merge_kernel(x_ref, gate_ref, ya_ref, yb_ref, yc_ref, wg_ref, bg_ref, wa_ref, wb_ref, wc_ref, wo_ref, o_ref):
    d = x_ref.shape[1]
    y = jax.nn.gelu(ya_ref[...])
    ya = y * jax.nn.sigmoid(_dot(y.astype(BF16), wg_ref[...]) + bg_ref[...])
    merged = (gate_ref[:, 0:d].astype(F32) * _dot(ya.astype(BF16), wa_ref[...])
              + gate_ref[:, d:2 * d].astype(F32) * _dot(yb_ref[...], wb_ref[...])
              + gate_ref[:, 2 * d:3 * d].astype(F32) * _dot(yc_ref[...], wc_ref[...]))
    o_ref[...] = x_ref[...] + _dot(merged.astype(BF16), wo_ref[...])


def _merge(x2, gate, ya, yb, yc, w_glu, b_glu, w_a, w_b, w_c, w_o, tm):
    m, d = x2.shape
    row = lambda a: pl.BlockSpec((tm, a.shape[1]), lambda i: (i, 0))
    weights = (w_glu, b_glu, w_a, w_b, w_c, w_o)
    return pl.pallas_call(
        _merge_kernel,
        grid=(m // tm,),
        in_specs=[row(a) for a in (x2, gate, ya, yb, yc)] + [_resident(w.shape) for w in weights],
        out_specs=pl.BlockSpec((tm, d), lambda i: (i, 0)),
        out_shape=jax.ShapeDtypeStruct((m, d), F32),
        compiler_params=_cparams(("parallel",)),
        name="branch_merge",
    )(x2, gate, ya, yb, yc, *weights)


def _ffn_kernel(x_ref, nw_ref, wu_ref, wd_ref, fw_ref, o_ref, *, hidden, chunk, final_norm):
    x = x_ref[...]
    h = _rms(x, nw_ref[...]).astype(BF16)
    acc = x
    for c0 in range(0, hidden, chunk):
        a = _dot(h, wu_ref[:, c0:c0 + chunk])
        b = _dot(h, wu_ref[:, hidden + c0:hidden + c0 + chunk])
        acc = acc + _dot((jax.nn.silu(a) * b).astype(BF16), wd_ref[c0:c0 + chunk, :])
    o_ref[...] = _rms(acc, fw_ref[...]) if final_norm else acc


def _ffn(x2, norm_w, w_up, w_down, final_w, final_norm, tm):
    m, d = x2.shape
    hidden = w_down.shape[0]
    chunk = hidden // 2 if hidden % (2 * LANES) == 0 else hidden
    return pl.pallas_call(
        functools.partial(_ffn_kernel, hidden=hidden, chunk=chunk, final_norm=final_norm),
        grid=(m // tm,),
        in_specs=[pl.BlockSpec((tm, d), lambda i: (i, 0)), _resident((1, d)), _resident(w_up.shape),
                  _resident(w_down.shape), _resident((1, d))],
        out_specs=pl.BlockSpec((tm, d), lambda i: (i, 0)),
        out_shape=jax.ShapeDtypeStruct((m, d), F32),
        compiler_params=_cparams(("parallel",)),
        name="swiglu_ffn",
    )(x2, norm_w.reshape(1, d), w_up, w_down, final_w.reshape(1, d))


def _layer(x2, bsz, seq, pos_tables, ssm_s0, conv_buf, attend, lw, lam_init, ssm_steps, independent,
           final_w, final_norm, tm, tm_wide, conv_tm, heads, layer, depth, kv_prev):
    gate, q, kf, kh, vf, vh, u, xb, xc, xin = _in_projection(
        x2, lw['norm_mix'], lw['w_in'], pos_tables[0], pos_tables[1], lw['widths'], tm, heads, layer, depth, kv_prev)
    ypre, s_new = _ssm_mixer(u.reshape(bsz, seq, -1), ssm_s0, lw['ssm_ops'][ssm_steps], ssm_steps, independent)
    sub = (lw['subln'].astype(F32) * (1.0 - lam_init)).reshape(1, -1)
    yb = attend(q, kh, vh, lw['lam'], sub)
    cw = xb.shape[1]
    yc, conv_new = _short_conv(xb.reshape(bsz, seq, cw), xc.reshape(bsz, seq, cw), xin.reshape(bsz, seq, cw),
                               conv_buf, lw['conv_w'], conv_tm)
    x2 = _merge(x2, gate, ypre.reshape(bsz * seq, -1), yb, yc.reshape(bsz * seq, cw), lw['w_glu'], lw['b_glu'],
                lw['w_br_ssm'], lw['w_br_attn'], lw['w_br_conv'], lw['w_out'], tm_wide)
    x2 = _ffn(x2, lw['norm_ffn'], lw['w_ffn_up'], lw['w_ffn_down'], final_w, final_norm, tm_wide)
    return x2, kf, vf, s_new, conv_new


def kernel(x_prompt, x_sample, cache_k, cache_v, state_ssm, state_conv, page_table, norm_mix, w_in, ssm_a_re, ssm_a_im, ssm_log_dt, ssm_b_re, ssm_b_im, ssm_c_re, ssm_c_im, ssm_d, w_glu, b_glu, lambda_q1, lambda_k1, lambda_q2, lambda_k2, subln, conv_w, w_br_ssm, w_br_attn, w_br_conv, w_out, norm_ffn, w_ffn_up, w_ffn_down, norm_final):
    bsz, seq, d = x_prompt.shape
    dbsz, dseq, _ = x_sample.shape
    depth = w_in.shape[0]
    heads, kwid = cache_k.shape[3], cache_k.shape[4]
    head_dim = kwid // 2
    vdim = cache_v.shape[4]
    assert kwid == LANES and vdim == LANES, "head blocks must fill one 128-lane block"
    past = page_table.shape[1] * cache_k.shape[2]
    ssm_w, conv_width = w_glu.shape[1], conv_w.shape[2]
    widths = (N_BRANCH * d, heads * kwid, heads * kwid, heads * vdim, ssm_w, conv_width)
    assert sum(widths) + 2 * conv_width == w_in.shape[2]
    assert seq % SSM_CHUNK == 0

    tm_p = min(256, bsz * seq)
    tm_s = dbsz * dseq
    scale = head_dim ** -0.5 * LOG2E
    pos_p = jnp.arange(seq, dtype=jnp.int32)
    pos_s = jnp.tile(past + jnp.arange(dseq, dtype=jnp.int32), dbsz)
    tabs_p = (_rope_tables(pos_p, head_dim, scale), _rope_tables(pos_p, head_dim, 1.0))
    tabs_s = (_rope_tables(pos_s, head_dim, scale), _rope_tables(pos_s, head_dim, 1.0))

    ssm0_p = jnp.zeros((bsz, 2) + state_ssm.shape[3:], F32)
    conv0_p = jnp.zeros((bsz,) + state_conv.shape[2:], F32)
    hp, hs = x_prompt.reshape(bsz * seq, d), x_sample.reshape(dbsz * dseq, d)
    outs = [[] for _ in range(4)]
    kv_p = tuple(jnp.zeros((depth * bsz * seq * heads, LANES), F32) for _ in range(2))
    kv_s = tuple(jnp.zeros((depth * dbsz * dseq * heads, LANES), F32) for _ in range(2))
    for l in range(depth):
        lam_init = 0.8 - 0.6 * math.exp(-0.3 * l)
        lam = (jnp.exp(jnp.sum(lambda_q1[l].astype(F32) * lambda_k1[l].astype(F32)))
               - jnp.exp(jnp.sum(lambda_q2[l].astype(F32) * lambda_k2[l].astype(F32))) + lam_init).reshape(1)
        ssm_args = (ssm_a_re[l], ssm_a_im[l], ssm_log_dt[l], ssm_b_re[l], ssm_b_im[l], ssm_c_re[l], ssm_c_im[l],
                    ssm_d[l])
        lw = {'norm_mix': norm_mix[l], 'w_in': w_in[l].astype(BF16), 'widths': widths, 'lam': lam,
              'ssm_ops': {steps: _ssm_operators(*ssm_args, steps) for steps in sorted({SSM_CHUNK, dseq})},
              'w_glu': w_glu[l].astype(BF16), 'b_glu': b_glu[l].astype(F32).reshape(1, -1),
              'subln': subln[l], 'conv_w': conv_w[l], 'w_br_ssm': w_br_ssm[l].astype(BF16),
              'w_br_attn': w_br_attn[l].astype(BF16), 'w_br_conv': w_br_conv[l].astype(BF16),
              'w_out': w_out[l].astype(BF16), 'norm_ffn': norm_ffn[l], 'w_ffn_up': w_ffn_up[l].astype(BF16),
              'w_ffn_down': w_ffn_down[l].astype(BF16)}
        last = l == depth - 1
        attend_p = functools.partial(_prompt_attention, bsz=bsz, seq=seq, heads=heads)
        hp, kp, vp, sp, cp = _layer(hp, bsz, seq, tabs_p, ssm0_p, conv0_p, attend_p, lw, lam_init, SSM_CHUNK, False,
                                    norm_final, last, tm_p, min(512, bsz * seq), min(512, seq), heads, l, depth, kv_p)
        attend_s = functools.partial(_sample_attention, cache_k=cache_k, cache_v=cache_v, page_table=page_table,
                                     layer=l, bsz=dbsz, nq=dseq, heads=heads)
        hs, ks, vs, ss, cs = _layer(hs, dbsz, dseq, tabs_s, state_ssm[l], state_conv[l], attend_s, lw, lam_init,
                                    dseq, True, norm_final, last, tm_s, tm_s, dseq, heads, l, depth, kv_s)
        kv_p, kv_s = (kp, vp), (ks, vs)
        for lst, val in zip(outs, (sp, ss, cp, cs)):
            lst.append(val)
    return ((hp.reshape(bsz, seq, d), hs.reshape(dbsz, dseq, d),
             kv_p[0].reshape(depth, bsz, seq, heads, kwid), kv_p[1].reshape(depth, bsz, seq, heads, vdim),
             kv_s[0].reshape(depth, dbsz, dseq, heads, kwid), kv_s[1].reshape(depth, dbsz, dseq, heads, vdim))
            + tuple(jnp.stack(o) for o in outs))
```

```python
import functools
import math

import jax
import jax.numpy as jnp
from jax import lax
from jax.experimental import pallas as pl
from jax.experimental.pallas import tpu as pltpu

RMS_EPS = 1e-6
ROPE_THETA = 10000.0
N_BRANCH = 3
SSM_GROUP = 16
SSM_CHUNK = 16
SSM_ROWS = 256
LANES = 128
SUBLANES = 8
GROUPS_PER_BLOCK = LANES // SSM_GROUP
VMEM_LIMIT = 56 * 1024 * 1024
PAGES_PER_STEP = 16
ATTN_BQ = 1024
ATTN_BK = 512
LOG2E = math.log2(math.e)

F32 = jnp.float32
BF16 = jnp.bfloat16


def _cparams(sem):
    return pltpu.CompilerParams(dimension_semantics=sem, vmem_limit_bytes=VMEM_LIMIT)


def _resident(shape):
    nd = len(shape)
    return pl.BlockSpec(shape, lambda *_: (0,) * nd, pipeline_mode=pl.Buffered(1))


def _rms(x, g):
    return x * lax.rsqrt(jnp.mean(x * x, axis=-1, keepdims=True) + RMS_EPS) * g


def _dot(a, b):
    return jnp.dot(a, b, preferred_element_type=F32)


def _dot_nt(a, b):
    return lax.dot_general(a, b, (((1,), (1,)), ((), ())), preferred_element_type=F32)


def _rope_block(x, cos, sa, sb):
    quarter = x.shape[1] // 4
    up = pltpu.roll(x, x.shape[1] - quarter, 1)
    down = pltpu.roll(x, quarter, 1)
    return x * cos + up * sa + down * sb


def _inproj_kernel(x_ref, nw_ref, w_ref, cq_ref, qa_ref, qb_ref, ck_ref, ka_ref, kb_ref, *refs, widths, chunk, heads):
    gate_ref, q_ref, kf_ref, kh_ref, vf_ref, vh_ref, u_ref, xb_ref, xc_ref, xin_ref = refs[-10:]
    tm = x_ref.shape[0]
    h = _rms(x_ref[...], nw_ref[...]).astype(BF16)
    gw, qw, kw, vw, uw, cw = widths

    def segment(col0, width, emit):
        for c0 in range(0, width, chunk):
            cs = min(chunk, width - c0)
            emit(c0, cs, _dot(h, w_ref[:, col0 + c0:col0 + c0 + cs]))

    def emit_gate(c0, cs, acc):
        gate_ref[:, c0:c0 + cs] = jax.nn.sigmoid(acc).astype(BF16)

    def emit_q(c0, cs, acc):
        for b0 in range(0, cs, LANES):
            r = _rope_block(acc[:, b0:b0 + LANES], cq_ref[...], qa_ref[...], qb_ref[...])
            q_ref[:, c0 + b0:c0 + b0 + LANES] = r.astype(BF16)

    def emit_k(c0, cs, acc):
        for b0 in range(0, cs, LANES):
            r = _rope_block(acc[:, b0:b0 + LANES], ck_ref[...], ka_ref[...], kb_ref[...])
            kf_ref[pl.ds((c0 + b0) // LANES, tm, stride=heads), :] = r
            kh_ref[:, c0 + b0:c0 + b0 + LANES] = r.astype(BF16)

    def emit_v(c0, cs, acc):
        for b0 in range(0, cs, LANES):
            vf_ref[pl.ds((c0 + b0) // LANES, tm, stride=heads), :] = acc[:, b0:b0 + LANES]
        vh_ref[:, c0:c0 + cs] = acc.astype(BF16)

    def emit_to(ref):
        def emit(c0, cs, acc):
            ref[:, c0:c0 + cs] = acc
        return emit

    col = 0
    for width, emit in ((gw, emit_gate), (qw, emit_q), (kw, emit_k), (vw, emit_v), (uw, emit_to(u_ref)),
                        (cw, emit_to(xb_ref)), (cw, emit_to(xc_ref)), (cw, emit_to(xin_ref))):
        segment(col, width, emit)
        col += width


def _in_projection(x2, norm_w, w_in_bf, rope_q, rope_k, widths, tm, heads, layer, depth, kv_prev):
    m, d = x2.shape
    gw, qw, kw, vw, uw, cw = widths
    ntab = rope_q[0].shape[0] // tm
    nt = m // tm
    row = lambda w: pl.BlockSpec((tm, w), lambda i: (i, 0))
    tab = pl.BlockSpec((tm, LANES), lambda i: (i % ntab, 0))
    kv_rows = pl.BlockSpec((tm * heads, LANES), lambda i: (layer * nt + i, 0))
    kv_shape = jax.ShapeDtypeStruct((depth * m * heads, LANES), F32)
    out_shape = (jax.ShapeDtypeStruct((m, gw), BF16), jax.ShapeDtypeStruct((m, qw), BF16),
                 kv_shape, jax.ShapeDtypeStruct((m, kw), BF16), kv_shape, jax.ShapeDtypeStruct((m, vw), BF16),
                 jax.ShapeDtypeStruct((m, uw), F32), jax.ShapeDtypeStruct((m, cw), F32),
                 jax.ShapeDtypeStruct((m, cw), F32), jax.ShapeDtypeStruct((m, cw), F32))
    n_in = 9
    return pl.pallas_call(
        functools.partial(_inproj_kernel, widths=widths, chunk=1024, heads=heads),
        grid=(nt,),
        in_specs=[row(d), _resident((1, d)), _resident(w_in_bf.shape), tab, tab, tab, tab, tab, tab]
                 + [pl.BlockSpec(memory_space=pl.ANY)] * 2,
        out_specs=(row(gw), row(qw), kv_rows, row(kw), kv_rows, row(vw), row(uw), row(cw), row(cw), row(cw)),
        out_shape=out_shape,
        input_output_aliases={n_in: 2, n_in + 1: 4},
        compiler_params=_cparams(("parallel",)),
        name="in_projection",
    )(x2, norm_w.reshape(1, d), w_in_bf, *rope_q, *rope_k, *kv_prev)


def _rope_tables(pos, head_dim, scale):
    half = head_dim // 2
    inv = ROPE_THETA ** (-jnp.arange(half, dtype=F32) / half)
    ang = pos.astype(F32)[:, None] * inv[None, :]
    cos, sin = jnp.cos(ang) * scale, jnp.sin(ang) * scale
    zero = jnp.zeros_like(sin)
    cos_t = jnp.concatenate([cos, cos, cos, cos], axis=1)
    sa = jnp.concatenate([-sin, zero, -sin, zero], axis=1)
    sb = jnp.concatenate([zero, sin, zero, sin], axis=1)
    return cos_t, sa, sb


def _ssm_kernel(u_ref, dk_ref, bm_ref, cm_ref, at_ref, d_ref, s0_ref, y_ref, sl_ref, m_ref, e_ref, st_ref, s_ref,
                *, steps, independent):
    seg = pl.program_id(2)
    half = at_ref.shape[-1] // 2

    @pl.when((pl.program_id(1) == 0) & (seg == 0))
    def _():
        m_ref[...] = jnp.zeros(m_ref.shape, BF16)
        for a in range(steps):
            for b in range(a, steps):
                m_ref[a * LANES:(a + 1) * LANES, b * LANES:(b + 1) * LANES] = dk_ref[0, b - a]

    x = jnp.concatenate([u_ref[0, :, t, :] for t in range(steps)], axis=1)
    xh = x.astype(BF16)
    e = _dot(xh, bm_ref[0])
    ar, ai = at_ref[0, :, :half], at_ref[0, :, half:]
    if independent:
        s0 = s0_ref[0, 0]
        sr, si = s0[:, :half], s0[:, half:]
        st = s0
        sl_ref[0, 0, :, :half] = ar * sr - ai * si + e[:, :half]
        sl_ref[0, 0, :, half:] = ar * si + ai * sr + e[:, half:]
    else:
        @pl.when(seg == 0)
        def _():
            s_ref[...] = s0_ref[0, 0]

        e_ref[...] = e
        nc = e_ref.shape[0]

        def body(c, carry):
            sr, si = carry
            st_ref[pl.ds(c, 1), :half] = sr
            st_ref[pl.ds(c, 1), half:] = si
            ec = e_ref[pl.ds(c, 1), :]
            return ar * sr - ai * si + ec[:, :half], ar * si + ai * sr + ec[:, half:]

        sr, si = lax.fori_loop(0, nc, body, (s_ref[:, :half], s_ref[:, half:]), unroll=8)
        s_ref[:, :half] = sr
        s_ref[:, half:] = si
        st = st_ref[...]

        @pl.when(seg == pl.num_programs(2) - 1)
        def _():
            sl_ref[0, 0] = s_ref[...]

    w = 2 * LANES
    intra = jnp.concatenate([_dot(xh[:, :c0 + w], m_ref[:c0 + w, c0:c0 + w]) for c0 in range(0, steps * LANES, w)],
                            axis=1)
    y = intra + _dot_nt(st.astype(BF16), cm_ref[0]) + d_ref[0] * x
    for t in range(steps):
        y_ref[0, :, t, :] = y[:, t * LANES:(t + 1) * LANES]


def _ssm_operators(a_re, a_im, log_dt, b_re, b_im, c_re, c_im, d_skip, steps):
    hp = lax.Precision.HIGHEST
    g, p = a_re.shape
    nblk = g // GROUPS_PER_BLOCK
    gl = GROUPS_PER_BLOCK
    lr, li = a_re.astype(F32), a_im.astype(F32)
    dt = jnp.exp(log_dt.astype(F32))[:, None]
    mag = jnp.exp(lr * dt)
    abar_r, abar_i = mag * jnp.cos(li * dt), mag * jnp.sin(li * dt)
    den = lr * lr + li * li
    nr, ni = abar_r - 1.0, abar_i
    fr, fi = (nr * lr + ni * li) / den, (ni * lr - nr * li) / den
    br, bi = b_re.astype(F32), b_im.astype(F32)
    bbar_r = fr[..., None] * br - fi[..., None] * bi
    bbar_i = fr[..., None] * bi + fi[..., None] * br
    n = jnp.arange(steps + 1, dtype=F32)[:, None, None]
    pw_mag = jnp.exp(n * (lr * dt))
    pw_r, pw_i = pw_mag * jnp.cos(n * (li * dt)), pw_mag * jnp.sin(n * (li * dt))
    cr, ci = c_re.astype(F32), c_im.astype(F32)
    wr = pw_r[..., None] * bbar_r - pw_i[..., None] * bbar_i
    wi = pw_r[..., None] * bbar_i + pw_i[..., None] * bbar_r
    kern = (jnp.einsum('gop,tgpi->tgio', cr, wr[:steps], precision=hp)
            - jnp.einsum('gop,tgpi->tgio', ci, wi[:steps], precision=hp))

    def group_diagonal(x):
        t, w = x.shape[0], x.shape[-1]
        x = jnp.tile(x.reshape(t, nblk, LANES, w), (1, 1, 1, gl))
        row_g = jnp.arange(LANES)[:, None] // SSM_GROUP
        col_g = jnp.arange(gl * w)[None, :] // w
        return jnp.swapaxes(jnp.where(row_g == col_g, x, 0.0), 0, 1).astype(BF16)

    dk = group_diagonal(kern)
    rev = steps - 1 - jnp.arange(steps)
    b_op = jnp.concatenate([group_diagonal(jnp.swapaxes(wr[rev], 2, 3)), group_diagonal(jnp.swapaxes(wi[rev], 2, 3))],
                           axis=3).reshape(nblk, steps * LANES, 2 * gl * p)
    qr = cr[None] * pw_r[1:, :, None, :] - ci[None] * pw_i[1:, :, None, :]
    qi = -(cr[None] * pw_i[1:, :, None, :] + ci[None] * pw_r[1:, :, None, :])
    c_op = jnp.concatenate([group_diagonal(qr), group_diagonal(qi)], axis=3).reshape(nblk, steps * LANES, 2 * gl * p)
    a_t = jnp.concatenate([pw_r[steps].reshape(nblk, 1, gl * p), pw_i[steps].reshape(nblk, 1, gl * p)], axis=2)
    d_op = jnp.tile(d_skip.astype(F32).reshape(nblk, 1, LANES), (1, 1, steps))
    return dk, b_op, c_op, a_t, d_op


def _state_to_blocks(s, nblk):
    lead = s.shape[:-3]
    g, p = s.shape[-2:]
    s = s.reshape(lead + (2, nblk, (g // nblk) * p))
    return jnp.concatenate([s[..., 0, :, :], s[..., 1, :, :]], axis=-1)


def _blocks_to_state(s, g, p):
    lead = s.shape[:-2]
    nblk = s.shape[-2]
    half = s.shape[-1] // 2
    s = jnp.stack([s[..., :half], s[..., half:]], axis=-3)
    return s.reshape(lead + (2, g, p))


def _ssm_mixer(u, s0, ops, steps, independent):
    m_op, b_op, c_op, a_t, d_op = ops
    bsz, seq, width = u.shape
    g, p = s0.shape[-2:]
    nblk = width // LANES
    sw = a_t.shape[-1]
    if independent:
        u4 = u.reshape(1, bsz, steps, width)
        s0b = jnp.swapaxes(_state_to_blocks(s0, nblk), 0, 1)[None]
        nb, nc, ncs = 1, bsz, bsz
    else:
        nc = seq // steps
        ncs = min(nc, SSM_ROWS)
        u4 = u.reshape(bsz, nc, steps, width)
        s0b = _state_to_blocks(s0, nblk)[:, :, None, :]
        nb = bsz
    srows = s0b.shape[2]
    blk = lambda a: pl.BlockSpec((1,) + a.shape[1:], lambda o, b, s: (o,) + (0,) * (a.ndim - 1),
                                 pipeline_mode=pl.Buffered(1))
    y4, sl = pl.pallas_call(
        functools.partial(_ssm_kernel, steps=steps, independent=independent),
        grid=(nblk, nb, nc // ncs),
        in_specs=[pl.BlockSpec((1, ncs, steps, LANES), lambda o, b, s: (b, s, 0, o)),
                  blk(m_op), blk(b_op), blk(c_op), blk(a_t), blk(d_op),
                  pl.BlockSpec((1, 1, srows, sw), lambda o, b, s: (b, o, 0, 0))],
        out_specs=(pl.BlockSpec((1, ncs, steps, LANES), lambda o, b, s: (b, s, 0, o)),
                   pl.BlockSpec((1, 1, srows, sw), lambda o, b, s: (b, o, 0, 0))),
        out_shape=(jax.ShapeDtypeStruct(u4.shape, F32), jax.ShapeDtypeStruct(s0b.shape, F32)),
        scratch_shapes=[pltpu.VMEM((steps * LANES, steps * LANES), BF16), pltpu.VMEM((ncs, sw), F32),
                        pltpu.VMEM((ncs, sw), F32), pltpu.VMEM((1, sw), F32)],
        compiler_params=_cparams(("arbitrary", "arbitrary", "arbitrary")),
        name="ssm_mixer",
    )(u4, m_op, b_op, c_op, a_t, d_op, s0b)
    if independent:
        s_last = _blocks_to_state(jnp.swapaxes(sl[0], 0, 1), g, p)
    else:
        s_last = _blocks_to_state(sl[:, :, 0, :], g, p)
    return y4.reshape(bsz, seq, width), s_last


def _conv_kernel(xb_ref, xc_ref, xin_ref, buf_ref, w_ref, y_ref, new_ref, z_ref, *, taps):
    i = pl.program_id(1)
    tm = xb_ref.shape[1]
    keep = taps - 1
    z = xc_ref[0] * xin_ref[0]

    @pl.when(i == 0)
    def _():
        z_ref[SUBLANES - keep:SUBLANES, :] = buf_ref[0]

    @pl.when(i > 0)
    def _():
        z_ref[SUBLANES - keep:SUBLANES, :] = z_ref[tm + SUBLANES - keep:tm + SUBLANES, :]

    z_ref[SUBLANES:, :] = z
    conv = w_ref[taps - 1:taps, :] * z
    for j in range(taps - 1):
        conv = conv + w_ref[j:j + 1, :] * z_ref[SUBLANES - keep + j:SUBLANES - keep + j + tm, :]
    y_ref[0] = (xb_ref[0] * conv).astype(BF16)

    @pl.when(i == pl.num_programs(1) - 1)
    def _():
        new_ref[0] = z_ref[tm + SUBLANES - keep:tm + SUBLANES, :]


def _short_conv(xb, xc, xin, buf, conv_w, tm):
    bsz, seq, width = xb.shape
    taps = conv_w.shape[0]
    row = pl.BlockSpec((1, tm, width), lambda b, i: (b, i, 0))
    st = pl.BlockSpec((1, taps - 1, width), lambda b, i: (b, 0, 0))
    return pl.pallas_call(
        functools.partial(_conv_kernel, taps=taps),
        grid=(bsz, seq // tm),
        in_specs=[row, row, row, st, pl.BlockSpec((taps, width), lambda b, i: (0, 0))],
        out_specs=(row, st),
        out_shape=(jax.ShapeDtypeStruct((bsz, seq, width), BF16), jax.ShapeDtypeStruct(buf.shape, F32)),
        scratch_shapes=[pltpu.VMEM((tm + SUBLANES, width), F32)],
        compiler_params=_cparams(("arbitrary", "arbitrary")),
        name="short_conv",
    )(xb, xc, xin, buf.astype(F32), conv_w.astype(F32))


def _finish_heads(o1, o2, lam, sub):
    od = o1 - lam * o2
    return od * lax.rsqrt(jnp.mean(od * od, axis=-1, keepdims=True) + RMS_EPS) * sub


def _prompt_attn_kernel(lam_ref, q_ref, k_ref, v_ref, sub_ref, o_ref, vx_ref, qq_ref, s_ref, m_ref, acc_ref,
                        *, bq, bk):
    qi = pl.program_id(2)
    hw = v_ref.shape[2]

    @pl.when(qi == 0)
    def _():
        vx_ref[:, :hw] = v_ref[0]
        vx_ref[:, hw:] = jnp.ones((vx_ref.shape[0], hw), BF16)

    q = q_ref[0]
    lane = lax.broadcasted_iota(jnp.int32, q.shape, 1)
    first = lane < (q.shape[1] // 2)
    zero = jnp.zeros_like(q)
    q1, q2 = jnp.where(first, q, zero), jnp.where(first, zero, q)
    qq_ref[...] = jnp.concatenate([q1[:bk], q2[:bk], q1[bk:], q2[bk:]], axis=0)

    def scores(j, slot):
        start = pl.multiple_of(j * bk, bk)
        s_ref[slot] = _dot_nt(qq_ref[...], k_ref[0, pl.ds(start, bk), :])

    def absorb(j, s, r0=0, triangular=False):
        start = pl.multiple_of(j * bk, bk)
        rows = pl.ds(r0, s.shape[0])
        if triangular:
            r = lax.broadcasted_iota(jnp.int32, s.shape, 0)
            r = jnp.where(r >= bk, r - bk, r)
            s = jnp.where(lax.broadcasted_iota(jnp.int32, s.shape, 1) <= r, s, -jnp.inf)
        m = m_ref[rows, :]
        m_new = jnp.maximum(m, jnp.max(s, axis=1, keepdims=True))
        m_ref[rows, :] = m_new
        p = jnp.exp2(s - jnp.concatenate([m_new] * (bk // LANES), axis=1)).astype(BF16)
        corr = jnp.exp2(m - m_new)
        acc_ref[rows, :] = (acc_ref[rows, :] * jnp.concatenate([corr] * (2 * hw // LANES), axis=1)
                            + _dot(p, vx_ref[pl.ds(start, bk), :]))

    def body(jj, _):
        j = 2 * jj
        scores(j + 1, 1)
        absorb(j, s_ref[0])
        scores(j + 2, 0)
        absorb(j + 1, s_ref[1])
        return 0

    m_ref[...] = jnp.full(m_ref.shape, -jnp.inf, F32)
    acc_ref[...] = jnp.zeros(acc_ref.shape, F32)
    scores(0, 0)
    lax.fori_loop(0, qi, body, 0)
    diag0 = 2 * qi
    absorb(diag0, s_ref[0, :bq, :], 0, triangular=True)
    absorb(diag0, s_ref[0, bq:, :], bq)
    start = pl.multiple_of((diag0 + 1) * bk, bk)
    absorb(diag0 + 1, _dot_nt(qq_ref[bq:, :], k_ref[0, pl.ds(start, bk), :]), bq, triangular=True)
    acc = acc_ref[...]
    o = acc[:, :hw] / acc[:, hw:hw + 1]
    for half in range(2):
        o1, o2 = o[half * bq:half * bq + bk], o[half * bq + bk:(half + 1) * bq]
        o_ref[0, half * bk:(half + 1) * bk, :] = _finish_heads(o1, o2, lam_ref[0], sub_ref[...]).astype(BF16)


def _prompt_attention(q, k, v, lam, sub, bsz, seq, heads):
    hw = q.shape[1] // heads
    bq, bk = min(ATTN_BQ, seq), min(ATTN_BK, seq)
    assert bq == 2 * bk, "the key sweep and the diagonal handling work on pairs of key blocks"
    q3, k3, v3 = (a.reshape(bsz, seq, heads * hw) for a in (q, k, v))
    whole = pl.BlockSpec((1, seq, hw), lambda b, h, i: (b, 0, h))
    tile = pl.BlockSpec((1, bq, hw), lambda b, h, i: (b, i, h))
    out = pl.pallas_call(
        functools.partial(_prompt_attn_kernel, bq=bq, bk=bk),
        grid=(bsz, heads, seq // bq),
        in_specs=[pl.BlockSpec(memory_space=pltpu.SMEM), tile, whole, whole,
                  pl.BlockSpec((1, hw), lambda b, h, i: (0, 0))],
        out_specs=tile,
        out_shape=jax.ShapeDtypeStruct((bsz, seq, heads * hw), BF16),
        scratch_shapes=[pltpu.VMEM((seq, 2 * hw), BF16), pltpu.VMEM((2 * bq, hw), BF16),
                        pltpu.VMEM((2, 2 * bq, bk), F32), pltpu.VMEM((2 * bq, LANES), F32),
                        pltpu.VMEM((2 * bq, 2 * hw), F32)],
        compiler_params=_cparams(("parallel", "parallel", "arbitrary")),
        name="prompt_attention",
    )(lam, q3, k3, v3, sub)
    return out.reshape(bsz * seq, heads * hw)


def _sample_attn_kernel(pt_ref, lam_ref, qa_ref, kn_ref, vn_ref, sub_ref, bias_ref, biasn_ref, *refs,
                        pages, heads, nq):
    k_refs, v_refs = refs[:pages], refs[pages:2 * pages]
    o_ref, m_ref, l_ref, acc_ref = refs[2 * pages:]
    j = pl.program_id(1)
    qa = qa_ref[0]

    @pl.when(j == 0)
    def _():
        m_ref[...] = jnp.full(m_ref.shape, -jnp.inf, F32)
        l_ref[...] = jnp.zeros(l_ref.shape, F32)
        acc_ref[...] = jnp.zeros(acc_ref.shape, F32)

    def update(state, scores, values):
        m, l, acc = state
        nblk = scores[0].shape[1] // LANES
        blocks = [s[:, i * LANES:(i + 1) * LANES] for s in scores for i in range(nblk)]
        part = blocks[0]
        for blk in blocks[1:]:
            part = jnp.maximum(part, blk)
        m_new = jnp.maximum(m, jnp.max(part, axis=1, keepdims=True))
        corr = jnp.exp2(m - m_new)
        l, acc = l * corr, acc * corr
        m_wide = jnp.concatenate([m_new] * nblk, axis=1)
        for s, v in zip(scores, values):
            p = jnp.exp2(s - m_wide)
            for i in range(nblk):
                l = l + p[:, i * LANES:(i + 1) * LANES]
            acc = acc + _dot(p.astype(BF16), v)
        return m_new, l, acc

    state = update((m_ref[...], l_ref[...], acc_ref[...]),
                   [_dot_nt(qa, k_refs[g][0, 0].astype(BF16)) + bias_ref[...] for g in range(pages)],
                   [v_refs[g][0, 0].astype(BF16) for g in range(pages)])
    m_ref[...], l_ref[...], acc_ref[...] = state

    @pl.when(j == pl.num_programs(1) - 1)
    def _():
        _, l, acc = update(state, [_dot_nt(qa, kn_ref[0]) + biasn_ref[...]], [vn_ref[0]])
        o = acc / jnp.sum(l, axis=1, keepdims=True)
        lam = lam_ref[0]
        for h in range(heads):
            o1 = o[(2 * h) * nq:(2 * h + 1) * nq]
            o2 = o[(2 * h + 1) * nq:(2 * h + 2) * nq]
            o_ref[0, :, h * LANES:(h + 1) * LANES] = _finish_heads(o1, o2, lam, sub_ref[...]).astype(BF16)


def _sample_attention(q, k, v, lam, sub, cache_k, cache_v, page_table, layer, bsz, nq, heads):
    hw = q.shape[1] // heads
    depth, n_pool, page = cache_k.shape[:3]
    n_pages = page_table.shape[1]
    pages = math.gcd(PAGES_PER_STEP, n_pages)
    rows = heads * 2 * nq
    prow = page * heads
    q4 = q.reshape(bsz, nq, heads, 2, hw // 2)
    sel = jnp.eye(2, dtype=q.dtype)
    qa = jnp.einsum('bqhcd,ce->bhcqed', q4, sel).reshape(bsz, rows, hw)
    pad = ((0, 0), (0, prow - nq * heads), (0, 0))
    kn = jnp.pad(k.reshape(bsz, nq * heads, hw), pad)
    vn = jnp.pad(v.reshape(bsz, nq * heads, hw), pad)
    ck = cache_k.reshape(depth, n_pool, prow, hw)
    cv = cache_v.reshape(depth, n_pool, prow, hw)
    pt = page_table.reshape(-1).astype(jnp.int32)
    assert hw == LANES and nq * heads <= prow
    r = jnp.arange(rows, dtype=jnp.int32)[:, None]
    c = jnp.arange(prow, dtype=jnp.int32)[None, :]
    same_head = (r // (2 * nq)) == (c % heads)
    bias = jnp.where(same_head, 0.0, -jnp.inf).astype(F32)
    bias_new = jnp.where(same_head & ((c // heads) <= (r % nq)), 0.0, -jnp.inf).astype(F32)

    def page_spec(g):
        return pl.BlockSpec((1, 1, prow, hw), lambda b, j, p: (layer, p[b * n_pages + j * pages + g], 0, 0))

    per_b = lambda n: pl.BlockSpec((1, n, hw), lambda b, j, p: (b, 0, 0))
    const = lambda shape: pl.BlockSpec(shape, lambda b, j, p: (0, 0))
    grid_spec = pltpu.PrefetchScalarGridSpec(
        num_scalar_prefetch=1,
        grid=(bsz, n_pages // pages),
        in_specs=[pl.BlockSpec(memory_space=pltpu.SMEM), per_b(rows), per_b(prow), per_b(prow),
                  const((1, hw)), const((rows, prow)), const((rows, prow))]
                 + [page_spec(g) for g in range(pages)] + [page_spec(g) for g in range(pages)],
        out_specs=pl.BlockSpec((1, nq, heads * hw), lambda b, j, p: (b, 0, 0)),
        scratch_shapes=[pltpu.VMEM((rows, LANES), F32), pltpu.VMEM((rows, LANES), F32),
                        pltpu.VMEM((rows, hw), F32)],
    )
    out = pl.pallas_call(
        functools.partial(_sample_attn_kernel, pages=pages, heads=heads, nq=nq),
        grid_spec=grid_spec,
        out_shape=jax.ShapeDtypeStruct((bsz, nq, heads * hw), BF16),
        compiler_params=_cparams(("parallel", "arbitrary")),
        name="sample_attention",
    )(pt, lam, qa, kn, vn, sub, bias, bias_new, *([ck] * pages), *([cv] * pages))
    return out.reshape(bsz * nq, heads * hw)


def _merge_kernel(x_ref, gate_ref, ya_ref, yb_ref, yc_ref, wg_ref, bg_ref, wa_ref, wb_ref, wc_ref, wo_ref, o_ref):
    d = x_ref.shape[1]
    y = jax.nn.gelu(ya_ref[...])
    ya = y * jax.nn.sigmoid(_dot(y.astype(BF16), wg_ref[...]) + bg_ref[...])
    merged = (gate_ref[:, 0:d].astype(F32) * _dot(ya.astype(BF16), wa_ref[...])
              + gate_ref[:, d:2 * d].astype(F32) * _dot(yb_ref[...], wb_ref[...])
              + gate_ref[:, 2 * d:3 * d].astype(F32) * _dot(yc_ref[...], wc_ref[...]))
    o_ref[...] = x_ref[...] + _dot(merged.astype(BF16), wo_ref[...])


def _merge(x2, gate, ya, yb, yc, w_glu, b_glu, w_a, w_b, w_c, w_o, tm):
    m, d = x2.shape
    row = lambda a: pl.BlockSpec((tm, a.shape[1]), lambda i: (i, 0))
    weights = (w_glu, b_glu, w_a, w_b, w_c, w_o)
    return pl.pallas_call(
        _merge_kernel,
        grid=(m // tm,),
        in_specs=[row(a) for a in (x2, gate, ya, yb, yc)] + [_resident(w.shape) for w in weights],
        out_specs=pl.BlockSpec((tm, d), lambda i: (i, 0)),
        out_shape=jax.ShapeDtypeStruct((m, d), F32),
        compiler_params=_cparams(("parallel",)),
        name="branch_merge",
    )(x2, gate, ya, yb, yc, *weights)


def _ffn_kernel(x_ref, nw_ref, wu_ref, wd_ref, fw_ref, o_ref, *, hidden, chunk, final_norm):
    x = x_ref[...]
    h = _rms(x, nw_ref[...]).astype(BF16)
    acc = x
    for c0 in range(0, hidden, chunk):
        a = _dot(h, wu_ref[:, c0:c0 + chunk])
        b = _dot(h, wu_ref[:, hidden + c0:hidden + c0 + chunk])
        acc = acc + _dot((jax.nn.silu(a) * b).astype(BF16), wd_ref[c0:c0 + chunk, :])
    o_ref[...] = _rms(acc, fw_ref[...]) if final_norm else acc


def _ffn(x2, norm_w, w_up, w_down, final_w, final_norm, tm):
    m, d = x2.shape
    hidden = w_down.shape[0]
    chunk = hidden // 2 if hidden % (2 * LANES) == 0 else hidden
    return pl.pallas_call(
        functools.partial(_ffn_kernel, hidden=hidden, chunk=chunk, final_norm=final_norm),
        grid=(m // tm,),
        in_specs=[pl.BlockSpec((tm, d), lambda i: (i, 0)), _resident((1, d)), _resident(w_up.shape),
                  _resident(w_down.shape), _resident((1, d))],
        out_specs=pl.BlockSpec((tm, d), lambda i: (i, 0)),
        out_shape=jax.ShapeDtypeStruct((m, d), F32),
        compiler_params=_cparams(("parallel",)),
        name="swiglu_ffn",
    )(x2, norm_w.reshape(1, d), w_up, w_down, final_w.reshape(1, d))


def _layer(x2, bsz, seq, pos_tables, ssm_s0, conv_buf, attend, lw, lam_init, ssm_steps, independent,
           final_w, final_norm, tm, tm_wide, conv_tm, heads, layer, depth, kv_prev):
    gate, q, kf, kh, vf, vh, u, xb, xc, xin = _in_projection(
        x2, lw['norm_mix'], lw['w_in'], pos_tables[0], pos_tables[1], lw['widths'], tm, heads, layer, depth, kv_prev)
    ypre, s_new = _ssm_mixer(u.reshape(bsz, seq, -1), ssm_s0, lw['ssm_ops'][ssm_steps], ssm_steps, independent)
    sub = (lw['subln'].astype(F32) * (1.0 - lam_init)).reshape(1, -1)
    yb = attend(q, kh, vh, lw['lam'], sub)
    cw = xb.shape[1]
    yc, conv_new = _short_conv(xb.reshape(bsz, seq, cw), xc.reshape(bsz, seq, cw), xin.reshape(bsz, seq, cw),
                               conv_buf, lw['conv_w'], conv_tm)
    x2 = _merge(x2, gate, ypre.reshape(bsz * seq, -1), yb, yc.reshape(bsz * seq, cw), lw['w_glu'], lw['b_glu'],
                lw['w_br_ssm'], lw['w_br_attn'], lw['w_br_conv'], lw['w_out'], tm_wide)
    x2 = _ffn(x2, lw['norm_ffn'], lw['w_ffn_up'], lw['w_ffn_down'], final_w, final_norm, tm_wide)
    return x2, kf, vf, s_new, conv_new


def kernel(x_prompt, x_sample, cache_k, cache_v, state_ssm, state_conv, page_table, norm_mix, w_in, ssm_a_re, ssm_a_im, ssm_log_dt, ssm_b_re, ssm_b_im, ssm_c_re, ssm_c_im, ssm_d, w_glu, b_glu, lambda_q1, lambda_k1, lambda_q2, lambda_k2, subln, conv_w, w_br_ssm, w_br_attn, w_br_conv, w_out, norm_ffn, w_ffn_up, w_ffn_down, norm_final):
    bsz, seq, d = x_prompt.shape
    dbsz, dseq, _ = x_sample.shape
    depth = w_in.shape[0]
    heads, kwid = cache_k.shape[3], cache_k.shape[4]
    head_dim = kwid // 2
    vdim = cache_v.shape[4]
    assert kwid == LANES and vdim == LANES, "head blocks must fill one 128-lane block"
    past = page_table.shape[1] * cache_k.shape[2]
    ssm_w, conv_width = w_glu.shape[1], conv_w.shape[2]
    widths = (N_BRANCH * d, heads * kwid, heads * kwid, heads * vdim, ssm_w, conv_width)
    assert sum(widths) + 2 * conv_width == w_in.shape[2]
    assert seq % SSM_CHUNK == 0

    tm_p = min(256, bsz * seq)
    tm_s = dbsz * dseq
    scale = head_dim ** -0.5 * LOG2E
    pos_p = jnp.arange(seq, dtype=jnp.int32)
    pos_s = jnp.tile(past + jnp.arange(dseq, dtype=jnp.int32), dbsz)
    tabs_p = (_rope_tables(pos_p, head_dim, scale), _rope_tables(pos_p, head_dim, 1.0))
    tabs_s = (_rope_tables(pos_s, head_dim, scale), _rope_tables(pos_s, head_dim, 1.0))

    ssm0_p = jnp.zeros((bsz, 2) + state_ssm.shape[3:], F32)
    conv0_p = jnp.zeros((bsz,) + state_conv.shape[2:], F32)
    hp, hs = x_prompt.reshape(bsz * seq, d), x_sample.reshape(dbsz * dseq, d)
    outs = [[] for _ in range(4)]
    kv_p = tuple(jnp.zeros((depth * bsz * seq * heads, LANES), F32) for _ in range(2))
    kv_s = tuple(jnp.zeros((depth * dbsz * dseq * heads, LANES), F32) for _ in range(2))
    for l in range(depth):
        lam_init = 0.8 - 0.6 * math.exp(-0.3 * l)
        lam = (jnp.exp(jnp.sum(lambda_q1[l].astype(F32) * lambda_k1[l].astype(F32)))
               - jnp.exp(jnp.sum(lambda_q2[l].astype(F32) * lambda_k2[l].astype(F32))) + lam_init).reshape(1)
        ssm_args = (ssm_a_re[l], ssm_a_im[l], ssm_log_dt[l], ssm_b_re[l], ssm_b_im[l], ssm_c_re[l], ssm_c_im[l],
                    ssm_d[l])
        lw = {'norm_mix': norm_mix[l], 'w_in': w_in[l].astype(BF16), 'widths': widths, 'lam': lam,
              'ssm_ops': {steps: _ssm_operators(*ssm_args, steps) for steps in sorted({SSM_CHUNK, dseq})},
              'w_glu': w_glu[l].astype(BF16), 'b_glu': b_glu[l].astype(F32).reshape(1, -1),
              'subln': subln[l], 'conv_w': conv_w[l], 'w_br_ssm': w_br_ssm[l].astype(BF16),
              'w_br_attn': w_br_attn[l].astype(BF16), 'w_br_conv': w_br_conv[l].astype(BF16),
              'w_out': w_out[l].astype(BF16), 'norm_ffn': norm_ffn[l], 'w_ffn_up': w_ffn_up[l].astype(BF16),
              'w_ffn_down': w_ffn_down[l].astype(BF16)}
        last = l == depth - 1
        attend_p = functools.partial(_prompt_attention, bsz=bsz, seq=seq, heads=heads)
        hp, kp, vp, sp, cp = _layer(hp, bsz, seq, tabs_p, ssm0_p, conv0_p, attend_p, lw, lam_init, SSM_CHUNK, False,
                                    norm_final, last, tm_p, min(512, bsz * seq), min(512, seq), heads, l, depth, kv_p)
        attend_s = functools.partial(_sample_attention, cache_k=cache_k, cache_v=cache_v, page_table=page_table,
                                     layer=l, bsz=dbsz, nq=dseq, heads=heads)
        hs, ks, vs, ss, cs = _layer(hs, dbsz, dseq, tabs_s, state_ssm[l], state_conv[l], attend_s, lw, lam_init,
                                    dseq, True, norm_final, last, tm_s, tm_s, dseq, heads, l, depth, kv_s)
        kv_p, kv_s = (kp, vp), (ks, vs)
        for lst, val in zip(outs, (sp, ss, cp, cs)):
            lst.append(val)
    return ((hp.reshape(bsz, seq, d), hs.reshape(dbsz, dseq, d),
             kv_p[0].reshape(depth, bsz, seq, heads, kwid), kv_p[1].reshape(depth, bsz, seq, heads, vdim),
             kv_s[0].reshape(depth, dbsz, dseq, heads, kwid), kv_s[1].reshape(depth, dbsz, dseq, heads, vdim))
            + tuple(jnp.stack(o) for o in outs))
```

```python
import functools
import math

import jax
import jax.numpy as jnp
from jax import lax
from jax.experimental import pallas as pl
from jax.experimental.pallas import tpu as pltpu

RMS_EPS = 1e-6
ROPE_THETA = 10000.0
N_BRANCH = 3
SSM_GROUP = 16
SSM_CHUNK = 16
SSM_ROWS = 256
LANES = 128
SUBLANES = 8
GROUPS_PER_BLOCK = LANES // SSM_GROUP
VMEM_LIMIT = 56 * 1024 * 1024
PAGES_PER_STEP = 16
ATTN_BQ = 1024
ATTN_BK = 512
LOG2E = math.log2(math.e)

F32 = jnp.float32
BF16 = jnp.bfloat16


def _cparams(sem):
    return pltpu.CompilerParams(dimension_semantics=sem, vmem_limit_bytes=VMEM_LIMIT)


def _resident(shape):
    nd = len(shape)
    return pl.BlockSpec(shape, lambda *_: (0,) * nd, pipeline_mode=pl.Buffered(1))


def _rms(x, g):
    return x * lax.rsqrt(jnp.mean(x * x, axis=-1, keepdims=True) + RMS_EPS) * g


def _dot(a, b):
    return jnp.dot(a, b, preferred_element_type=F32)


def _dot_nt(a, b):
    return lax.dot_general(a, b, (((1,), (1,)), ((), ())), preferred_element_type=F32)


def _rope_block(x, cos, sa, sb):
    quarter = x.shape[1] // 4
    up = pltpu.roll(x, x.shape[1] - quarter, 1)
    down = pltpu.roll(x, quarter, 1)
    return x * cos + up * sa + down * sb


def _inproj_kernel(x_ref, nw_ref, w_ref, cq_ref, qa_ref, qb_ref, ck_ref, ka_ref, kb_ref, *refs, widths, chunk, heads):
    gate_ref, q_ref, kf_ref, kh_ref, vf_ref, vh_ref, u_ref, xb_ref, xc_ref, xin_ref = refs[-10:]
    tm = x_ref.shape[0]
    h = _rms(x_ref[...], nw_ref[...]).astype(BF16)
    gw, qw, kw, vw, uw, cw = widths

    def segment(col0, width, emit):
        for c0 in range(0, width, chunk):
            cs = min(chunk, width - c0)
            emit(c0, cs, _dot(h, w_ref[:, col0 + c0:col0 + c0 + cs]))

    def emit_gate(c0, cs, acc):
        gate_ref[:, c0:c0 + cs] = jax.nn.sigmoid(acc).astype(BF16)

    def emit_q(c0, cs, acc):
        for b0 in range(0, cs, LANES):
            r = _rope_block(acc[:, b0:b0 + LANES], cq_ref[...], qa_ref[...], qb_ref[...])
            q_ref[:, c0 + b0:c0 + b0 + LANES] = r.astype(BF16)

    def emit_k(c0, cs, acc):
        for b0 in range(0, cs, LANES):
            r = _rope_block(acc[:, b0:b0 + LANES], ck_ref[...], ka_ref[...], kb_ref[...])
            kf_ref[pl.ds((c0 + b0) // LANES, tm, stride=heads), :] = r
            kh_ref[:, c0 + b0:c0 + b0 + LANES] = r.astype(BF16)

    def emit_v(c0, cs, acc):
        for b0 in range(0, cs, LANES):
            vf_ref[pl.ds((c0 + b0) // LANES, tm, stride=heads), :] = acc[:, b0:b0 + LANES]
        vh_ref[:, c0:c0 + cs] = acc.astype(BF16)

    def emit_to(ref):
        def emit(c0, cs, acc):
            ref[:, c0:c0 + cs] = acc
        return emit

    col = 0
    for width, emit in ((gw, emit_gate), (qw, emit_q), (kw, emit_k), (vw, emit_v), (uw, emit_to(u_ref)),
                        (cw, emit_to(xb_ref)), (cw, emit_to(xc_ref)), (cw, emit_to(xin_ref))):
        segment(col, width, emit)
        col += width


def _in_projection(x2, norm_w, w_in_bf, rope_q, rope_k, widths, tm, heads, layer, depth, kv_prev):
    m, d = x2.shape
    gw, qw, kw, vw, uw, cw = widths
    ntab = rope_q[0].shape[0] // tm
    nt = m // tm
    row = lambda w: pl.BlockSpec((tm, w), lambda i: (i, 0))
    tab = pl.BlockSpec((tm, LANES), lambda i: (i % ntab, 0))
    kv_rows = pl.BlockSpec((tm * heads, LANES), lambda i: (layer * nt + i, 0))
    kv_shape = jax.ShapeDtypeStruct((depth * m * heads, LANES), F32)
    out_shape = (jax.ShapeDtypeStruct((m, gw), BF16), jax.ShapeDtypeStruct((m, qw), BF16),
                 kv_shape, jax.ShapeDtypeStruct((m, kw), BF16), kv_shape, jax.ShapeDtypeStruct((m, vw), BF16),
                 jax.ShapeDtypeStruct((m, uw), F32), jax.ShapeDtypeStruct((m, cw), F32),
                 jax.ShapeDtypeStruct((m, cw), F32), jax.ShapeDtypeStruct((m, cw), F32))
    n_in = 9
    return pl.pallas_call(
        functools.partial(_inproj_kernel, widths=widths, chunk=1024, heads=heads),
        grid=(nt,),
        in_specs=[row(d), _resident((1, d)), _resident(w_in_bf.shape), tab, tab, tab, tab, tab, tab]
                 + [pl.BlockSpec(memory_space=pl.ANY)] * 2,
        out_specs=(row(gw), row(qw), kv_rows, row(kw), kv_rows, row(vw), row(uw), row(cw), row(cw), row(cw)),
        out_shape=out_shape,
        input_output_aliases={n_in: 2, n_in + 1: 4},
        compiler_params=_cparams(("parallel",)),
        name="in_projection",
    )(x2, norm_w.reshape(1, d), w_in_bf, *rope_q, *rope_k, *kv_prev)


def _rope_tables(pos, head_dim, scale):
    half = head_dim // 2
    inv = ROPE_THETA ** (-jnp.arange(half, dtype=F32) / half)
    ang = pos.astype(F32)[:, None] * inv[None, :]
    cos, sin = jnp.cos(ang) * scale, jnp.sin(ang) * scale
    zero = jnp.zeros_like(sin)
    cos_t = jnp.concatenate([cos, cos, cos, cos], axis=1)
    sa = jnp.concatenate([-sin, zero, -sin, zero], axis=1)
    sb = jnp.concatenate([zero, sin, zero, sin], axis=1)
    return cos_t, sa, sb


def _ssm_kernel(u_ref, dk_ref, bm_ref, cm_ref, at_ref, d_ref, s0_ref, y_ref, sl_ref, m_ref, e_ref, st_ref, s_ref,
                *, steps, independent):
    seg = pl.program_id(2)
    half = at_ref.shape[-1] // 2

    @pl.when((pl.program_id(1) == 0) & (seg == 0))
    def _():
        m_ref[...] = jnp.zeros(m_ref.shape, BF16)
        for a in range(steps):
            for b in range(a, steps):
                m_ref[a * LANES:(a + 1) * LANES, b * LANES:(b + 1) * LANES] = dk_ref[0, b - a]

    x = jnp.concatenate([u_ref[0, :, t, :] for t in range(steps)], axis=1)
    xh = x.astype(BF16)
    e = _dot(xh, bm_ref[0])
    ar, ai = at_ref[0, :, :half], at_ref[0, :, half:]
    if independent:
        s0 = s0_ref[0, 0]
        sr, si = s0[:, :half], s0[:, half:]
        st = s0
        sl_ref[0, 0, :, :half] = ar * sr - ai * si + e[:, :half]
        sl_ref[0, 0, :, half:] = ar * si + ai * sr + e[:, half:]
    else:
        @pl.when(seg == 0)
        def _():
            s_ref[...] = s0_ref[0, 0]

        e_ref[...] = e
        nc = e_ref.shape[0]

        def body(c, carry):
            sr, si = carry
            st_ref[pl.ds(c, 1), :half] = sr
            st_ref[pl.ds(c, 1), half:] = si
            ec = e_ref[pl.ds(c, 1), :]
            return ar * sr - ai * si + ec[:, :half], ar * si + ai * sr + ec[:, half:]

        sr, si = lax.fori_loop(0, nc, body, (s_ref[:, :half], s_ref[:, half:]), unroll=8)
        s_ref[:, :half] = sr
        s_ref[:, half:] = si
        st = st_ref[...]

        @pl.when(seg == pl.num_programs(2) - 1)
        def _():
            sl_ref[0, 0] = s_ref[...]

    w = 2 * LANES
    intra = jnp.concatenate([_dot(xh[:, :c0 + w], m_ref[:c0 + w, c0:c0 + w]) for c0 in range(0, steps * LANES, w)],
                            axis=1)
    y = intra + _dot_nt(st.astype(BF16), cm_ref[0]) + d_ref[0] * x
    for t in range(steps):
        y_ref[0, :, t, :] = y[:, t * LANES:(t + 1) * LANES]


def _ssm_operators(a_re, a_im, log_dt, b_re, b_im, c_re, c_im, d_skip, steps):
    hp = lax.Precision.HIGHEST
    g, p = a_re.shape
    nblk = g // GROUPS_PER_BLOCK
    gl = GROUPS_PER_BLOCK
    lr, li = a_re.astype(F32), a_im.astype(F32)
    dt = jnp.exp(log_dt.astype(F32))[:, None]
    mag = jnp.exp(lr * dt)
    abar_r, abar_i = mag * jnp.cos(li * dt), mag * jnp.sin(li * dt)
    den = lr * lr + li * li
    nr, ni = abar_r - 1.0, abar_i
    fr, fi = (nr * lr + ni * li) / den, (ni * lr - nr * li) / den
    br, bi = b_re.astype(F32), b_im.astype(F32)
    bbar_r = fr[..., None] * br - fi[..., None] * bi
    bbar_i = fr[..., None] * bi + fi[..., None] * br
    n = jnp.arange(steps + 1, dtype=F32)[:, None, None]
    pw_mag = jnp.exp(n * (lr * dt))
    pw_r, pw_i = pw_mag * jnp.cos(n * (li * dt)), pw_mag * jnp.sin(n * (li * dt))
    cr, ci = c_re.astype(F32), c_im.astype(F32)
    wr = pw_r[..., None] * bbar_r - pw_i[..., None] * bbar_i
    wi = pw_r[..., None] * bbar_i + pw_i[..., None] * bbar_r
    kern = (jnp.einsum('gop,tgpi->tgio', cr, wr[:steps], precision=hp)
            - jnp.einsum('gop,tgpi->tgio', ci, wi[:steps], precision=hp))

    def group_diagonal(x):
        t, w = x.shape[0], x.shape[-1]
        x = jnp.tile(x.reshape(t, nblk, LANES, w), (1, 1, 1, gl))
        row_g = jnp.arange(LANES)[:, None] // SSM_GROUP
        col_g = jnp.arange(gl * w)[None, :] // w
        return jnp.swapaxes(jnp.where(row_g == col_g, x, 0.0), 0, 1).astype(BF16)

    dk = group_diagonal(kern)
    rev = steps - 1 - jnp.arange(steps)
    b_op = jnp.concatenate([group_diagonal(jnp.swapaxes(wr[rev], 2, 3)), group_diagonal(jnp.swapaxes(wi[rev], 2, 3))],
                           axis=3).reshape(nblk, steps * LANES, 2 * gl * p)
    qr = cr[None] * pw_r[1:, :, None, :] - ci[None] * pw_i[1:, :, None, :]
    qi = -(cr[None] * pw_i[1:, :, None, :] + ci[None] * pw_r[1:, :, None, :])
    c_op = jnp.concatenate([group_diagonal(qr), group_diagonal(qi)], axis=3).reshape(nblk, steps * LANES, 2 * gl * p)
    a_t = jnp.concatenate([pw_r[steps].reshape(nblk, 1, gl * p), pw_i[steps].reshape(nblk, 1, gl * p)], axis=2)
    d_op = jnp.tile(d_skip.astype(F32).reshape(nblk, 1, LANES), (1, 1, steps))
    return dk, b_op, c_op, a_t, d_op


def _state_to_blocks(s, nblk):
    lead = s.shape[:-3]
    g, p = s.shape[-2:]
    s = s.reshape(lead + (2, nblk, (g // nblk) * p))
    return jnp.concatenate([s[..., 0, :, :], s[..., 1, :, :]], axis=-1)


def _blocks_to_state(s, g, p):
    lead = s.shape[:-2]
    nblk = s.shape[-2]
    half = s.shape[-1] // 2
    s = jnp.stack([s[..., :half], s[..., half:]], axis=-3)
    return s.reshape(lead + (2, g, p))


def _ssm_mixer(u, s0, ops, steps, independent):
    m_op, b_op, c_op, a_t, d_op = ops
    bsz, seq, width = u.shape
    g, p = s0.shape[-2:]
    nblk = width // LANES
    sw = a_t.shape[-1]
    if independent:
        u4 = u.reshape(1, bsz, steps, width)
        s0b = jnp.swapaxes(_state_to_blocks(s0, nblk), 0, 1)[None]
        nb, nc, ncs = 1, bsz, bsz
    else:
        nc = seq // steps
        ncs = min(nc, SSM_ROWS)
        u4 = u.reshape(bsz, nc, steps, width)
        s0b = _state_to_blocks(s0, nblk)[:, :, None, :]
        nb = bsz
    srows = s0b.shape[2]
    blk = lambda a: pl.BlockSpec((1,) + a.shape[1:], lambda o, b, s: (o,) + (0,) * (a.ndim - 1),
                                 pipeline_mode=pl.Buffered(1))
    y4, sl = pl.pallas_call(
        functools.partial(_ssm_kernel, steps=steps, independent=independent),
        grid=(nblk, nb, nc // ncs),
        in_specs=[pl.BlockSpec((1, ncs, steps, LANES), lambda o, b, s: (b, s, 0, o)),
                  blk(m_op), blk(b_op), blk(c_op), blk(a_t), blk(d_op),
                  pl.BlockSpec((1, 1, srows, sw), lambda o, b, s: (b, o, 0, 0))],
        out_specs=(pl.BlockSpec((1, ncs, steps, LANES), lambda o, b, s: (b, s, 0, o)),
                   pl.BlockSpec((1, 1, srows, sw), lambda o, b, s: (b, o, 0, 0))),
        out_shape=(jax.ShapeDtypeStruct(u4.shape, F32), jax.ShapeDtypeStruct(s0b.shape, F32)),
        scratch_shapes=[pltpu.VMEM((steps * LANES, steps * LANES), BF16), pltpu.VMEM((ncs, sw), F32),
                        pltpu.VMEM((ncs, sw), F32), pltpu.VMEM((1, sw), F32)],
        compiler_params=_cparams(("arbitrary", "arbitrary", "arbitrary")),
        name="ssm_mixer",
    )(u4, m_op, b_op, c_op, a_t, d_op, s0b)
    if independent:
        s_last = _blocks_to_state(jnp.swapaxes(sl[0], 0, 1), g, p)
    else:
        s_last = _blocks_to_state(sl[:, :, 0, :], g, p)
    return y4.reshape(bsz, seq, width), s_last


def _conv_kernel(xb_ref, xc_ref, xin_ref, buf_ref, w_ref, y_ref, new_ref, z_ref, *, taps):
    i = pl.program_id(1)
    tm = xb_ref.shape[1]
    keep = taps - 1
    z = xc_ref[0] * xin_ref[0]

    @pl.when(i == 0)
    def _():
        z_ref[SUBLANES - keep:SUBLANES, :] = buf_ref[0]

    @pl.when(i > 0)
    def _():
        z_ref[SUBLANES - keep:SUBLANES, :] = z_ref[tm + SUBLANES - keep:tm + SUBLANES, :]

    z_ref[SUBLANES:, :] = z
    conv = w_ref[taps - 1:taps, :] * z
    for j in range(taps - 1):
        conv = conv + w_ref[j:j + 1, :] * z_ref[SUBLANES - keep + j:SUBLANES - keep + j + tm, :]
    y_ref[0] = (xb_ref[0] * conv).astype(BF16)

    @pl.when(i == pl.num_programs(1) - 1)
    def _():
        new_ref[0] = z_ref[tm + SUBLANES - keep:tm + SUBLANES, :]


def _short_conv(xb, xc, xin, buf, conv_w, tm):
    bsz, seq, width = xb.shape
    taps = conv_w.shape[0]
    row = pl.BlockSpec((1, tm, width), lambda b, i: (b, i, 0))
    st = pl.BlockSpec((1, taps - 1, width), lambda b, i: (b, 0, 0))
    return pl.pallas_call(
        functools.partial(_conv_kernel, taps=taps),
        grid=(bsz, seq // tm),
        in_specs=[row, row, row, st, pl.BlockSpec((taps, width), lambda b, i: (0, 0))],
        out_specs=(row, st),
        out_shape=(jax.ShapeDtypeStruct((bsz, seq, width), BF16), jax.ShapeDtypeStruct(buf.shape, F32)),
        scratch_shapes=[pltpu.VMEM((tm + SUBLANES, width), F32)],
        compiler_params=_cparams(("arbitrary", "arbitrary")),
        name="short_conv",
    )(xb, xc, xin, buf.astype(F32), conv_w.astype(F32))


def _finish_heads(o1, o2, lam, sub):
    od = o1 - lam * o2
    return od * lax.rsqrt(jnp.mean(od * od, axis=-1, keepdims=True) + RMS_EPS) * sub


def _prompt_attn_kernel(lam_ref, q_ref, k_ref, v_ref, sub_ref, o_ref, vx_ref, qq_ref, s_ref, m_ref, acc_ref,
                        *, bq, bk):
    qi = pl.program_id(2)
    hw = v_ref.shape[2]

    def stack_queries(tile):
        q = q_ref[0, pl.ds(pl.multiple_of(tile * bq, bq), bq), :]
        first = lax.broadcasted_iota(jnp.int32, q.shape, 1) < (hw // 2)
        zero = jnp.zeros_like(q)
        q1, q2 = jnp.where(first, q, zero), jnp.where(first, zero, q)
        qq_ref[...] = jnp.concatenate([q1[:bk], q2[:bk], q1[bk:], q2[bk:]], axis=0)

    def scores(j, slot):
        start = pl.multiple_of(j * bk, bk)
        s_ref[slot] = _dot_nt(qq_ref[...], k_ref[0, pl.ds(start, bk), :])

    def absorb(j, s, r0=0, triangular=False):
        start = pl.multiple_of(j * bk, bk)
        rows = pl.ds(r0, s.shape[0])
        if triangular:
            r = lax.broadcasted_iota(jnp.int32, s.shape, 0)
            r = jnp.where(r >= bk, r - bk, r)
            s = jnp.where(lax.broadcasted_iota(jnp.int32, s.shape, 1) <= r, s, -jnp.inf)
        m = m_ref[rows, :]
        m_new = jnp.maximum(m, jnp.max(s, axis=1, keepdims=True))
        m_ref[rows, :] = m_new
        p = jnp.exp2(s - jnp.concatenate([m_new] * (bk // LANES), axis=1)).astype(BF16)
        corr = jnp.exp2(m - m_new)
        acc_ref[rows, :] = (acc_ref[rows, :] * jnp.concatenate([corr] * (2 * hw // LANES), axis=1)
                            + _dot(p, vx_ref[pl.ds(start, bk), :]))

    def body(jj, _):
        j = 2 * jj
        scores(j + 1, 1)
        absorb(j, s_ref[0])
        scores(j + 2, 0)
        absorb(j + 1, s_ref[1])
        return 0

    @pl.when(qi == 0)
    def _():
        vx_ref[:, :hw] = v_ref[0]
        vx_ref[:, hw:] = jnp.ones((vx_ref.shape[0], hw), BF16)
        stack_queries(0)
        scores(0, 0)

    m_ref[...] = jnp.full(m_ref.shape, -jnp.inf, F32)
    acc_ref[...] = jnp.zeros(acc_ref.shape, F32)
    lax.fori_loop(0, qi, body, 0)
    diag0 = 2 * qi
    absorb(diag0, s_ref[0, :bq, :], 0, triangular=True)
    absorb(diag0, s_ref[0, bq:, :], bq)
    start = pl.multiple_of((diag0 + 1) * bk, bk)
    absorb(diag0 + 1, _dot_nt(qq_ref[bq:, :], k_ref[0, pl.ds(start, bk), :]), bq, triangular=True)
    stack_queries(jnp.minimum(qi + 1, pl.num_programs(2) - 1))
    scores(0, 0)
    acc = acc_ref[...]
    o = acc[:, :hw] / acc[:, hw:hw + 1]
    for half in range(2):
        o1, o2 = o[half * bq:half * bq + bk], o[half * bq + bk:(half + 1) * bq]
        o_ref[0, half * bk:(half + 1) * bk, :] = _finish_heads(o1, o2, lam_ref[0], sub_ref[...]).astype(BF16)


def _prompt_attention(q, k, v, lam, sub, bsz, seq, heads):
    hw = q.shape[1] // heads
    bq, bk = min(ATTN_BQ, seq), min(ATTN_BK, seq)
    assert bq == 2 * bk, "the key sweep and the diagonal handling work on pairs of key blocks"
    q3, k3, v3 = (a.reshape(bsz, seq, heads * hw) for a in (q, k, v))
    whole = pl.BlockSpec((1, seq, hw), lambda b, h, i: (b, 0, h))
    tile = pl.BlockSpec((1, bq, hw), lambda b, h, i: (b, i, h))
    out = pl.pallas_call(
        functools.partial(_prompt_attn_kernel, bq=bq, bk=bk),
        grid=(bsz, heads, seq // bq),
        in_specs=[pl.BlockSpec(memory_space=pltpu.SMEM), whole, whole, whole,
                  pl.BlockSpec((1, hw), lambda b, h, i: (0, 0))],
        out_specs=tile,
        out_shape=jax.ShapeDtypeStruct((bsz, seq, heads * hw), BF16),
        scratch_shapes=[pltpu.VMEM((seq, 2 * hw), BF16), pltpu.VMEM((2 * bq, hw), BF16),
                        pltpu.VMEM((2, 2 * bq, bk), F32), pltpu.VMEM((2 * bq, LANES), F32),
                        pltpu.VMEM((2 * bq, 2 * hw), F32)],
        compiler_params=_cparams(("parallel", "parallel", "arbitrary")),
        name="prompt_attention",
    )(lam, q3, k3, v3, sub)
    return out.reshape(bsz * seq, heads * hw)


def _sample_attn_kernel(pt_ref, lam_ref, qa_ref, kn_ref, vn_ref, sub_ref, bias_ref, biasn_ref, *refs,
                        pages, heads, nq):
    k_refs, v_refs = refs[:pages], refs[pages:2 * pages]
    o_ref, m_ref, l_ref, acc_ref = refs[2 * pages:]
    j = pl.program_id(1)
    qa = qa_ref[0]

    @pl.when(j == 0)
    def _():
        m_ref[...] = jnp.full(m_ref.shape, -jnp.inf, F32)
        l_ref[...] = jnp.zeros(l_ref.shape, F32)
        acc_ref[...] = jnp.zeros(acc_ref.shape, F32)

    def update(state, scores, values):
        m, l, acc = state
        nblk = scores[0].shape[1] // LANES
        blocks = [s[:, i * LANES:(i + 1) * LANES] for s in scores for i in range(nblk)]
        part = blocks[0]
        for blk in blocks[1:]:
            part = jnp.maximum(part, blk)
        m_new = jnp.maximum(m, jnp.max(part, axis=1, keepdims=True))
        corr = jnp.exp2(m - m_new)
        l, acc = l * corr, acc * corr
        m_wide = jnp.concatenate([m_new] * nblk, axis=1)
        for s, v in zip(scores, values):
            p = jnp.exp2(s - m_wide)
            for i in range(nblk):
                l = l + p[:, i * LANES:(i + 1) * LANES]
            acc = acc + _dot(p.astype(BF16), v)
        return m_new, l, acc

    state = update((m_ref[...], l_ref[...], acc_ref[...]),
                   [_dot_nt(qa, k_refs[g][0, 0].astype(BF16)) + bias_ref[...] for g in range(pages)],
                   [v_refs[g][0, 0].astype(BF16) for g in range(pages)])
    m_ref[...], l_ref[...], acc_ref[...] = state

    @pl.when(j == pl.num_programs(1) - 1)
    def _():
        _, l, acc = update(state, [_dot_nt(qa, kn_ref[0]) + biasn_ref[...]], [vn_ref[0]])
        o = acc / jnp.sum(l, axis=1, keepdims=True)
        lam = lam_ref[0]
        for h in range(heads):
            o1 = o[(2 * h) * nq:(2 * h + 1) * nq]
            o2 = o[(2 * h + 1) * nq:(2 * h + 2) * nq]
            o_ref[0, :, h * LANES:(h + 1) * LANES] = _finish_heads(o1, o2, lam, sub_ref[...]).astype(BF16)


def _sample_attention(q, k, v, lam, sub, cache_k, cache_v, page_table, layer, bsz, nq, heads):
    hw = q.shape[1] // heads
    depth, n_pool, page = cache_k.shape[:3]
    n_pages = page_table.shape[1]
    pages = math.gcd(PAGES_PER_STEP, n_pages)
    rows = heads * 2 * nq
    prow = page * heads
    q4 = q.reshape(bsz, nq, heads, 2, hw // 2)
    sel = jnp.eye(2, dtype=q.dtype)
    qa = jnp.einsum('bqhcd,ce->bhcqed', q4, sel).reshape(bsz, rows, hw)
    pad = ((0, 0), (0, prow - nq * heads), (0, 0))
    kn = jnp.pad(k.reshape(bsz, nq * heads, hw), pad)
    vn = jnp.pad(v.reshape(bsz, nq * heads, hw), pad)
    ck = cache_k.reshape(depth, n_pool, prow, hw)
    cv = cache_v.reshape(depth, n_pool, prow, hw)
    pt = page_table.reshape(-1).astype(jnp.int32)
    assert hw == LANES and nq * heads <= prow
    r = jnp.arange(rows, dtype=jnp.int32)[:, None]
    c = jnp.arange(prow, dtype=jnp.int32)[None, :]
    same_head = (r // (2 * nq)) == (c % heads)
    bias = jnp.where(same_head, 0.0, -jnp.inf).astype(F32)
    bias_new = jnp.where(same_head & ((c // heads) <= (r % nq)), 0.0, -jnp.inf).astype(F32)

    def page_spec(g):
        return pl.BlockSpec((1, 1, prow, hw), lambda b, j, p: (layer, p[b * n_pages + j * pages + g], 0, 0))

    per_b = lambda n: pl.BlockSpec((1, n, hw), lambda b, j, p: (b, 0, 0))
    const = lambda shape: pl.BlockSpec(shape, lambda b, j, p: (0, 0))
    grid_spec = pltpu.PrefetchScalarGridSpec(
        num_scalar_prefetch=1,
        grid=(bsz, n_pages // pages),
        in_specs=[pl.BlockSpec(memory_space=pltpu.SMEM), per_b(rows), per_b(prow), per_b(prow),
                  const((1, hw)), const((rows, prow)), const((rows, prow))]
                 + [page_spec(g) for g in range(pages)] + [page_spec(g) for g in range(pages)],
        out_specs=pl.BlockSpec((1, nq, heads * hw), lambda b, j, p: (b, 0, 0)),
        scratch_shapes=[pltpu.VMEM((rows, LANES), F32), pltpu.VMEM((rows, LANES), F32),
                        pltpu.VMEM((rows, hw), F32)],
    )
    out = pl.pallas_call(
        functools.partial(_sample_attn_kernel, pages=pages, heads=heads, nq=nq),
        grid_spec=grid_spec,
        out_shape=jax.ShapeDtypeStruct((bsz, nq, heads * hw), BF16),
        compiler_params=_cparams(("parallel", "arbitrary")),
        name="sample_attention",
    )(pt, lam, qa, kn, vn, sub, bias, bias_new, *([ck] * pages), *([cv] * pages))
    return out.reshape(bsz * nq, heads * hw)


def _merge_kernel(x_ref, gate_ref, ya_ref, yb_ref, yc_ref, wg_ref, bg_ref, wa_ref, wb_ref, wc_ref, wo_ref, o_ref):
    d = x_ref.shape[1]
    y = jax.nn.gelu(ya_ref[...])
    ya = y * jax.nn.sigmoid(_dot(y.astype(BF16), wg_ref[...]) + bg_ref[...])
    merged = (gate_ref[:, 0:d].astype(F32) * _dot(ya.astype(BF16), wa_ref[...])
              + gate_ref[:, d:2 * d].astype(F32) * _dot(yb_ref[...], wb_ref[...])
              + gate_ref[:, 2 * d:3 * d].astype(F32) * _dot(yc_ref[...], wc_ref[...]))
    o_ref[...] = x_ref[...] + _dot(merged.astype(BF16), wo_ref[...])


def _merge(x2, gate, ya, yb, yc, w_glu, b_glu, w_a, w_b, w_c, w_o, tm):
    m, d = x2.shape
    row = lambda a: pl.BlockSpec((tm, a.shape[1]), lambda i: (i, 0))
    weights = (w_glu, b_glu, w_a, w_b, w_c, w_o)
    return pl.pallas_call(
        _merge_kernel,
        grid=(m // tm,),
        in_specs=[row(a) for a in (x2, gate, ya, yb, yc)] + [_resident(w.shape) for w in weights],
        out_specs=pl.BlockSpec((tm, d), lambda i: (i, 0)),
        out_shape=jax.ShapeDtypeStruct((m, d), F32),
        compiler_params=_cparams(("parallel",)),
        name="branch_merge",
    )(x2, gate, ya, yb, yc, *weights)


def _ffn_kernel(x_ref, nw_ref, wu_ref, wd_ref, fw_ref, o_ref, *, hidden, chunk, final_norm):
    x = x_ref[...]
    h = _rms(x, nw_ref[...]).astype(BF16)
    acc = x
    for c0 in range(0, hidden, chunk):
        a = _dot(h, wu_ref[:, c0:c0 + chunk])
        b = _dot(h, wu_ref[:, hidden + c0:hidden + c0 + chunk])
        acc = acc + _dot((jax.nn.silu(a) * b).astype(BF16), wd_ref[c0:c0 + chunk, :])
    o_ref[...] = _rms(acc, fw_ref[...]) if final_norm else acc


def _ffn(x2, norm_w, w_up, w_down, final_w, final_norm, tm):
    m, d = x2.shape
    hidden = w_down.shape[0]
    chunk = hidden // 2 if hidden % (2 * LANES) == 0 else hidden
    return pl.pallas_call(
        functools.partial(_ffn_kernel, hidden=hidden, chunk=chunk, final_norm=final_norm),
        grid=(m // tm,),
        in_specs=[pl.BlockSpec((tm, d), lambda i: (i, 0)), _resident((1, d)), _resident(w_up.shape),
                  _resident(w_down.shape), _resident((1, d))],
        out_specs=pl.BlockSpec((tm, d), lambda i: (i, 0)),
        out_shape=jax.ShapeDtypeStruct((m, d), F32),
        compiler_params=_cparams(("parallel",)),
        name="swiglu_ffn",
    )(x2, norm_w.reshape(1, d), w_up, w_down, final_w.reshape(1, d))


def _layer(x2, bsz, seq, pos_tables, ssm_s0, conv_buf, attend, lw, lam_init, ssm_steps, independent,
           final_w, final_norm, tm, tm_wide, conv_tm, heads, layer, depth, kv_prev):
    gate, q, kf, kh, vf, vh, u, xb, xc, xin = _in_projection(
        x2, lw['norm_mix'], lw['w_in'], pos_tables[0], pos_tables[1], lw['widths'], tm, heads, layer, depth, kv_prev)
    ypre, s_new = _ssm_mixer(u.reshape(bsz, seq, -1), ssm_s0, lw['ssm_ops'][ssm_steps], ssm_steps, independent)
    sub = (lw['subln'].astype(F32) * (1.0 - lam_init)).reshape(1, -1)
    yb = attend(q, kh, vh, lw['lam'], sub)
    cw = xb.shape[1]
    yc, conv_new = _short_conv(xb.reshape(bsz, seq, cw), xc.reshape(bsz, seq, cw), xin.reshape(bsz, seq, cw),
                               conv_buf, lw['conv_w'], conv_tm)
    x2 = _merge(x2, gate, ypre.reshape(bsz * seq, -1), yb, yc.reshape(bsz * seq, cw), lw['w_glu'], lw['b_glu'],
                lw['w_br_ssm'], lw['w_br_attn'], lw['w_br_conv'], lw['w_out'], tm_wide)
    x2 = _ffn(x2, lw['norm_ffn'], lw['w_ffn_up'], lw['w_ffn_down'], final_w, final_norm, tm_wide)
    return x2, kf, vf, s_new, conv_new


def kernel(x_prompt, x_sample, cache_k, cache_v, state_ssm, state_conv, page_table, norm_mix, w_in, ssm_a_re, ssm_a_im, ssm_log_dt, ssm_b_re, ssm_b_im, ssm_c_re, ssm_c_im, ssm_d, w_glu, b_glu, lambda_q1, lambda_k1, lambda_q2, lambda_k2, subln, conv_w, w_br_ssm, w_br_attn, w_br_conv, w_out, norm_ffn, w_ffn_up, w_ffn_down, norm_final):
    bsz, seq, d = x_prompt.shape
    dbsz, dseq, _ = x_sample.shape
    depth = w_in.shape[0]
    heads, kwid = cache_k.shape[3], cache_k.shape[4]
    head_dim = kwid // 2
    vdim = cache_v.shape[4]
    assert kwid == LANES and vdim == LANES, "head blocks must fill one 128-lane block"
    past = page_table.shape[1] * cache_k.shape[2]
    ssm_w, conv_width = w_glu.shape[1], conv_w.shape[2]
    widths = (N_BRANCH * d, heads * kwid, heads * kwid, heads * vdim, ssm_w, conv_width)
    assert sum(widths) + 2 * conv_width == w_in.shape[2]
    assert seq % SSM_CHUNK == 0

    tm_p = min(256, bsz * seq)
    tm_s = dbsz * dseq
    scale = head_dim ** -0.5 * LOG2E
    pos_p = jnp.arange(seq, dtype=jnp.int32)
    pos_s = jnp.tile(past + jnp.arange(dseq, dtype=jnp.int32), dbsz)
    tabs_p = (_rope_tables(pos_p, head_dim, scale), _rope_tables(pos_p, head_dim, 1.0))
    tabs_s = (_rope_tables(pos_s, head_dim, scale), _rope_tables(pos_s, head_dim, 1.0))

    ssm0_p = jnp.zeros((bsz, 2) + state_ssm.shape[3:], F32)
    conv0_p = jnp.zeros((bsz,) + state_conv.shape[2:], F32)
    hp, hs = x_prompt.reshape(bsz * seq, d), x_sample.reshape(dbsz * dseq, d)
    outs = [[] for _ in range(4)]
    kv_p = tuple(jnp.zeros((depth * bsz * seq * heads, LANES), F32) for _ in range(2))
    kv_s = tuple(jnp.zeros((depth * dbsz * dseq * heads, LANES), F32) for _ in range(2))
    for l in range(depth):
        lam_init = 0.8 - 0.6 * math.exp(-0.3 * l)
        lam = (jnp.exp(jnp.sum(lambda_q1[l].astype(F32) * lambda_k1[l].astype(F32)))
               - jnp.exp(jnp.sum(lambda_q2[l].astype(F32) * lambda_k2[l].astype(F32))) + lam_init).reshape(1)
        ssm_args = (ssm_a_re[l], ssm_a_im[l], ssm_log_dt[l], ssm_b_re[l], ssm_b_im[l], ssm_c_re[l], ssm_c_im[l],
                    ssm_d[l])
        lw = {'norm_mix': norm_mix[l], 'w_in': w_in[l].astype(BF16), 'widths': widths, 'lam': lam,
              'ssm_ops': {steps: _ssm_operators(*ssm_args, steps) for steps in sorted({SSM_CHUNK, dseq})},
              'w_glu': w_glu[l].astype(BF16), 'b_glu': b_glu[l].astype(F32).reshape(1, -1),
              'subln': subln[l], 'conv_w': conv_w[l], 'w_br_ssm': w_br_ssm[l].astype(BF16),
              'w_br_attn': w_br_attn[l].astype(BF16), 'w_br_conv': w_br_conv[l].astype(BF16),
              'w_out': w_out[l].astype(BF16), 'norm_ffn': norm_ffn[l], 'w_ffn_up': w_ffn_up[l].astype(BF16),
              'w_ffn_down': w_ffn_down[l].astype(BF16)}
        last = l == depth - 1
        attend_p = functools.partial(_prompt_attention, bsz=bsz, seq=seq, heads=heads)
        hp, kp, vp, sp, cp = _layer(hp, bsz, seq, tabs_p, ssm0_p, conv0_p, attend_p, lw, lam_init, SSM_CHUNK, False,
                                    norm_final, last, tm_p, min(512, bsz * seq), min(512, seq), heads, l, depth, kv_p)
        attend_s = functools.partial(_sample_attention, cache_k=cache_k, cache_v=cache_v, page_table=page_table,
                                     layer=l, bsz=dbsz, nq=dseq, heads=heads)
        hs, ks, vs, ss, cs = _layer(hs, dbsz, dseq, tabs_s, state_ssm[l], state_conv[l], attend_s, lw, lam_init,
                                    dseq, True, norm_final, last, tm_s, tm_s, dseq, heads, l, depth, kv_s)
        kv_p, kv_s = (kp, vp), (ks, vs)
        for lst, val in zip(outs, (sp, ss, cp, cs)):
            lst.append(val)
    return ((hp.reshape(bsz, seq, d), hs.reshape(dbsz, dseq, d),
             kv_p[0].reshape(depth, bsz, seq, heads, kwid), kv_p[1].reshape(depth, bsz, seq, heads, vdim),
             kv_s[0].reshape(depth, dbsz, dseq, heads, kwid), kv_s[1].reshape(depth, dbsz, dseq, heads, vdim))
            + tuple(jnp.stack(o) for o in outs))
```
